```python
import jax, jax.numpy as jnp
from jax import lax
import numpy as np

D_MODEL = 2048
BATCH = 4
SEQ = 4096
DEPTH = 2

RET_HEADS = 8
RET_DK = 64
RET_DV = 128
RET_CHUNK = 128
MLA_HEADS = 8
MLA_Q_RANK = 512
MLA_KV_RANK = 256
MLA_NOPE = 128
MLA_ROPE = 64
MLA_DV = 128
ATTN_BLOCK = 128
RET_WIDTH = RET_HEADS * RET_DV
MLA_WIDTH = MLA_HEADS * MLA_DV
MIX_WIDTH = RET_WIDTH + MLA_WIDTH
IN_SIZES = (RET_HEADS * RET_DK, RET_HEADS * RET_DK, RET_WIDTH, RET_WIDTH, MLA_Q_RANK, MLA_KV_RANK, MLA_ROPE)
IN_COLS = 2 * RET_HEADS * RET_DK + 2 * RET_WIDTH + MLA_Q_RANK + MLA_KV_RANK + MLA_ROPE
D_FF = 256 * ((8 * D_MODEL // 3 + 255) // 256)
ROPE_DIM = 64
ROPE_BASE = 10000.0
EPS = 1e-6
N_MOD = 9

kernel_name = "hybrid_retention_mla_macaron_adaln"


def rmsnorm(x, gain):
    xf = x.astype(jnp.float32)
    y = xf * lax.rsqrt(jnp.mean(xf * xf, axis=-1, keepdims=True) + EPS)
    return (y * gain.astype(jnp.float32)).astype(x.dtype)


def rope_tables(positions):
    inv = ROPE_BASE ** (-jnp.arange(0, ROPE_DIM, 2, dtype=jnp.float32) / ROPE_DIM)
    ang = positions.astype(jnp.float32)[..., None] * inv
    return jnp.cos(ang), jnp.sin(ang)


def apply_rope(x, cos, sin):
    x1, x2 = jnp.split(x, 2, axis=-1)
    c = cos[:, :, None, :].astype(x.dtype)
    s = sin[:, :, None, :].astype(x.dtype)
    return jnp.concatenate([x1 * c - x2 * s, x1 * s + x2 * c], axis=-1)


def modulate(h, shift, scale):
    return h * (1 + scale[:, None, :]) + shift[:, None, :]


def swiglu(h, w_gu, w_down):
    g, u = jnp.split(h @ w_gu, 2, axis=-1)
    return (jax.nn.silu(g) * u) @ w_down


def chunkwise_retention(q, k, v):
    B, S, H, dk = q.shape
    dv = v.shape[-1]
    C = RET_CHUNK
    N = S // C
    dt = q.dtype
    log_g = jnp.log1p(-(2.0 ** (-5.0 - jnp.arange(H, dtype=jnp.float32))))
    idx = jnp.arange(C, dtype=jnp.float32)
    rel = idx[:, None] - idx[None, :]
    intra_decay = jnp.where(rel >= 0, jnp.exp(log_g[:, None, None] * jnp.maximum(rel, 0.0)), 0.0)
    q_decay = jnp.exp(log_g[:, None] * (idx + 1.0))
    k_decay = jnp.exp(log_g[:, None] * (C - 1.0 - idx))
    chunk_decay = jnp.exp(log_g * C)

    qc = q.reshape(B, N, C, H, dk)
    kc = k.reshape(B, N, C, H, dk)
    vc = v.reshape(B, N, C, H, dv)
    s = jnp.einsum('bnihd,bnjhd->bnhij', qc, kc) * intra_decay.astype(dt)
    intra = jnp.einsum('bnhij,bnjhe->bnihe', s, vc)
    inc = jnp.einsum('bnjhd,hj,bnjhe->nbhde', kc, k_decay.astype(dt), vc)
    cd = chunk_decay.astype(dt)[None, :, None, None]

    def step(state, inc_n):
        return state * cd + inc_n, state

    _, prev = lax.scan(step, jnp.zeros((B, H, dk, dv), dt), inc)
    cross = jnp.einsum('bnihd,hi,nbhde->bnihe', qc, q_decay.astype(dt), prev)
    return (intra + cross).reshape(B, S, H, dv)


def causal_block_attention(q, k, v):
    B, S, H, dqk = q.shape
    dv = v.shape[-1]
    NB = S // ATTN_BLOCK
    scale = dqk ** -0.5
    qb = q.reshape(B, NB, ATTN_BLOCK, H, dqk).transpose(1, 0, 2, 3, 4)
    key_pos = jnp.arange(S)

    def one_block(args):
        q_blk, start = args
        sc = jnp.einsum('bqhd,bkhd->bhqk', q_blk, k, preferred_element_type=jnp.float32) * scale
        qpos = start + jnp.arange(ATTN_BLOCK)
        sc = jnp.where(key_pos[None, :] <= qpos[:, None], sc, -jnp.inf)
        p = jax.nn.softmax(sc, axis=-1).astype(v.dtype)
        return jnp.einsum('bhqk,bkhe->bqhe', p, v)

    out = lax.map(one_block, (qb, jnp.arange(NB) * ATTN_BLOCK))
    return out.transpose(1, 0, 2, 3, 4).reshape(B, S, H, dv)


def hybrid_mixer(h, cos, sin, w_in, ret_norm_g, q_norm_g, w_uq, kv_norm_g, w_ukv, w_out):
    B, S, _ = h.shape
    proj = h @ w_in
    cuts = [int(v) for v in np.cumsum(IN_SIZES)[:-1]]
    q_r, k_r, v_r, g_r, c_q, c_kv, k_pe = jnp.split(proj, cuts, axis=-1)

    q_r = apply_rope(q_r.reshape(B, S, RET_HEADS, RET_DK), cos, sin)
    k_r = apply_rope(k_r.reshape(B, S, RET_HEADS, RET_DK), cos, sin) * (RET_DK ** -0.5)
    v_r = v_r.reshape(B, S, RET_HEADS, RET_DV)
    y_r = chunkwise_retention(q_r, k_r, v_r)
    y_r = rmsnorm(y_r, ret_norm_g.reshape(RET_HEADS, RET_DV)).reshape(B, S, RET_WIDTH)
    y_r = jax.nn.silu(g_r) * y_r

    q = (rmsnorm(c_q, q_norm_g) @ w_uq).reshape(B, S, MLA_HEADS, MLA_NOPE + MLA_ROPE)
    q_nope, q_pe = jnp.split(q, [MLA_NOPE], axis=-1)
    q_pe = apply_rope(q_pe, cos, sin)
    kv = (rmsnorm(c_kv, kv_norm_g) @ w_ukv).reshape(B, S, MLA_HEADS, MLA_NOPE + MLA_DV)
    k_nope, v_m = jnp.split(kv, [MLA_NOPE], axis=-1)
    k_pe = apply_rope(k_pe[:, :, None, :], cos, sin)
    q_m = jnp.concatenate([q_nope, q_pe], axis=-1)
    k_m = jnp.concatenate([k_nope, jnp.broadcast_to(k_pe, (B, S, MLA_HEADS, MLA_ROPE))], axis=-1)
    y_m = causal_block_attention(q_m, k_m, v_m).reshape(B, S, MLA_WIDTH)

    return jnp.concatenate([y_r, y_m], axis=-1) @ w_out


def setup_inputs(seed: int = 0) -> dict:
    key = jax.random.key(seed)
    ks = jax.random.split(key, 24)
    f32 = jnp.float32

    def nrm(k, shape, scale):
        return jax.random.normal(k, shape, f32) * scale

    def gain(k, shape):
        return 1.0 + 0.02 * jax.random.normal(k, shape, f32)

    offsets = jax.random.randint(ks[2], (BATCH, 1), 0, 1024, dtype=jnp.int32)
    positions = offsets + jnp.arange(SEQ, dtype=jnp.int32)[None, :]
    return {
        "x": nrm(ks[0], (BATCH, SEQ, D_MODEL), 1.0),
        "c": nrm(ks[1], (BATCH, D_MODEL), 1.0),
        "positions": positions,
        "w_ada": nrm(ks[3], (DEPTH, D_MODEL, N_MOD * D_MODEL), 0.5 * D_MODEL ** -0.5),
        "b_ada": nrm(ks[4], (DEPTH, N_MOD * D_MODEL), 0.02),
        "norm_ffn1": gain(ks[5], (DEPTH, D_MODEL)),
        "ffn1_w_gu": nrm(ks[6], (DEPTH, D_MODEL, 2 * D_FF), D_MODEL ** -0.5),
        "ffn1_w_down": nrm(ks[7], (DEPTH, D_FF, D_MODEL), D_FF ** -0.5),
        "norm_mix": gain(ks[8], (DEPTH, D_MODEL)),
        "w_in": nrm(ks[9], (DEPTH, D_MODEL, IN_COLS), D_MODEL ** -0.5),
        "ret_norm_g": gain(ks[10], (DEPTH, RET_WIDTH)),
        "q_norm_g": gain(ks[11], (DEPTH, MLA_Q_RANK)),
        "w_uq": nrm(ks[12], (DEPTH, MLA_Q_RANK, MLA_HEADS * (MLA_NOPE + MLA_ROPE)), MLA_Q_RANK ** -0.5),
        "kv_norm_g": gain(ks[13], (DEPTH, MLA_KV_RANK)),
        "w_ukv": nrm(ks[14], (DEPTH, MLA_KV_RANK, MLA_HEADS * (MLA_NOPE + MLA_DV)), MLA_KV_RANK ** -0.5),
        "w_out": nrm(ks[15], (DEPTH, MIX_WIDTH, D_MODEL), MIX_WIDTH ** -0.5),
        "norm_ffn2": gain(ks[16], (DEPTH, D_MODEL)),
        "ffn2_w_gu": nrm(ks[17], (DEPTH, D_MODEL, 2 * D_FF), D_MODEL ** -0.5),
        "ffn2_w_down": nrm(ks[18], (DEPTH, D_FF, D_MODEL), D_FF ** -0.5),
        "final_norm": gain(ks[19], (D_MODEL,)),
    }


def reference(x, c, positions, w_ada, b_ada, norm_ffn1, ffn1_w_gu, ffn1_w_down, norm_mix, w_in,
              ret_norm_g, q_norm_g, w_uq, kv_norm_g, w_ukv, w_out, norm_ffn2, ffn2_w_gu,
              ffn2_w_down, final_norm):
    cos, sin = rope_tables(positions)
    c_act = jax.nn.silu(c)
    for l in range(DEPTH):
        mod = c_act @ w_ada[l] + b_ada[l]
        sh1, sc1, g1, sh2, sc2, g2, sh3, sc3, g3 = jnp.split(mod, N_MOD, axis=-1)
        h = modulate(rmsnorm(x, norm_ffn1[l]), sh1, sc1)
        x = x + 0.5 * g1[:, None, :] * swiglu(h, ffn1_w_gu[l], ffn1_w_down[l])
        h = modulate(rmsnorm(x, norm_mix[l]), sh2, sc2)
        x = x + g2[:, None, :] * hybrid_mixer(h, cos, sin, w_in[l], ret_norm_g[l], q_norm_g[l], w_uq[l],
                                              kv_norm_g[l], w_ukv[l], w_out[l])
        h = modulate(rmsnorm(x, norm_ffn2[l]), sh3, sc3)
        x = x + 0.5 * g3[:, None, :] * swiglu(h, ffn2_w_gu[l], ffn2_w_down[l])
    return rmsnorm(x, final_norm)
```

```python
import functools

import jax
import jax.numpy as jnp
import numpy as np
from jax import lax
from jax.experimental import pallas as pl
from jax.experimental.pallas import tpu as pltpu

D_MODEL = 2048
DEPTH = 2
RET_HEADS = 8
RET_DK = 64
RET_DV = 128
RET_CHUNK = 128
MLA_HEADS = 8
MLA_Q_RANK = 512
MLA_KV_RANK = 256
MLA_NOPE = 128
MLA_ROPE = 64
MLA_DV = 128
RET_WIDTH = RET_HEADS * RET_DV
MLA_WIDTH = MLA_HEADS * MLA_DV
ROPE_DIM = 64
ROPE_BASE = 10000.0
EPS = 1e-6
N_MOD = 9

LANES = 128
MLA_QK_PAD = 256
IN_QK = 2 * RET_HEADS * RET_DK
IN_V0 = IN_QK
IN_G0 = IN_V0 + RET_WIDTH
IN_CQ0 = IN_G0 + RET_WIDTH
IN_CKV0 = IN_CQ0 + MLA_Q_RANK
IN_KPE0 = IN_CKV0 + MLA_KV_RANK
IN_COLS_PAD = IN_KPE0 + LANES

VMEM_LIMIT = 56 * 1024 * 1024

BF16 = jnp.bfloat16
F32 = jnp.float32


def _rms(xf, gain):
    return xf * lax.rsqrt(jnp.mean(xf * xf, axis=-1, keepdims=True) + EPS) * gain


def _rope(blk, cos, sin_signed):
    lane = lax.broadcasted_iota(jnp.int32, blk.shape, 1)
    partner = jnp.where(lane % ROPE_DIM < ROPE_DIM // 2,
                        pltpu.roll(blk, LANES - ROPE_DIM // 2, 1),
                        pltpu.roll(blk, ROPE_DIM // 2, 1))
    return blk * cos + partner * sin_signed


def _mod_kernel(c_ref, w_ref, b_ref, o_ref):
    ca = jax.nn.silu(c_ref[...]).astype(BF16)
    o_ref[...] = jnp.dot(ca, w_ref[...].astype(BF16), preferred_element_type=F32) + b_ref[...]


def _adaln_mod(c_pad, w_ada, b_ada):
    depth, d, n = w_ada.shape
    rows = c_pad.shape[0]
    tn = 1024
    return pl.pallas_call(
        _mod_kernel,
        grid=(depth, n // tn),
        in_specs=[
            pl.BlockSpec((rows, d), lambda l, j: (0, 0)),
            pl.BlockSpec((None, d, tn), lambda l, j: (l, 0, j)),
            pl.BlockSpec((None, 1, tn), lambda l, j: (l, 0, j)),
        ],
        out_specs=pl.BlockSpec((None, rows, tn), lambda l, j: (l, 0, j)),
        out_shape=jax.ShapeDtypeStruct((depth, rows, n), F32),
        compiler_params=pltpu.CompilerParams(
            dimension_semantics=("arbitrary", "arbitrary"), vmem_limit_bytes=VMEM_LIMIT),
        name="adaln_mod",
    )(c_pad, w_ada, b_ada.reshape(depth, 1, n))


def _rope_table_kernel(pos_ref, inv_ref, sign_ref, cos_ref, sin_ref):
    ang = pos_ref[...].astype(F32) * inv_ref[...]
    cos_ref[...] = jnp.cos(ang)
    sin_ref[...] = jnp.sin(ang) * sign_ref[...]


def _rope_tables(positions):
    t = positions.size
    tm = 2048
    inv = ROPE_BASE ** (-jnp.arange(0, ROPE_DIM, 2, dtype=F32) / ROPE_DIM)
    inv_l = jnp.tile(inv, LANES // (ROPE_DIM // 2)).reshape(1, LANES)
    sign = np.where(np.arange(LANES) % ROPE_DIM < ROPE_DIM // 2, -1.0, 1.0).astype(np.float32)
    return pl.pallas_call(
        _rope_table_kernel,
        grid=(t // tm,),
        in_specs=[
            pl.BlockSpec((tm, 1), lambda i: (i, 0)),
            pl.BlockSpec((1, LANES), lambda i: (0, 0)),
            pl.BlockSpec((1, LANES), lambda i: (0, 0)),
        ],
        out_specs=[pl.BlockSpec((tm, LANES), lambda i: (i, 0))] * 2,
        out_shape=[jax.ShapeDtypeStruct((t, LANES), F32)] * 2,
        compiler_params=pltpu.CompilerParams(dimension_semantics=("arbitrary",)),
        name="rope_tables",
    )(positions.reshape(t, 1), inv_l, jnp.asarray(sign).reshape(1, LANES))


def _ffn_kernel(x_ref, mod_ref, gain_ref, wg_ref, wu_ref, wd_ref, fgain_ref, o_ref,
                h_sc, acc_sc, *, mod_base, final_norm):
    j = pl.program_id(1)

    @pl.when(j == 0)
    def _():
        y = _rms(x_ref[...], gain_ref[...])
        shift = mod_ref[mod_base:mod_base + 1, :]
        scale = mod_ref[mod_base + 1:mod_base + 2, :]
        h_sc[...] = (y * (1 + scale) + shift).astype(BF16)
        acc_sc[...] = jnp.zeros_like(acc_sc)

    h = h_sc[...]
    g = jnp.dot(h, wg_ref[...], preferred_element_type=F32)
    u = jnp.dot(h, wu_ref[...], preferred_element_type=F32)
    a = (jax.nn.silu(g) * u).astype(BF16)
    acc_sc[...] += jnp.dot(a, wd_ref[...], preferred_element_type=F32)

    @pl.when(j == pl.num_programs(1) - 1)
    def _():
        gate = mod_ref[mod_base + 2:mod_base + 3, :]
        out = x_ref[...] + 0.5 * gate * acc_sc[...]
        if final_norm:
            out = _rms(out, fgain_ref[...])
        o_ref[...] = out


def _ffn(x, mod_l, gain, w_gu, w_down, final_gain, *, mod_base, final_norm, seq):
    t, d = x.shape
    f = w_down.shape[0]
    tm, tf = 512, 512
    nf = f // tf
    kern = functools.partial(_ffn_kernel, mod_base=mod_base, final_norm=final_norm)
    return pl.pallas_call(
        kern,
        grid=(t // tm, nf),
        in_specs=[
            pl.BlockSpec((tm, d), lambda i, j: (i, 0)),
            pl.BlockSpec((None, N_MOD, d), lambda i, j: (i // (seq // tm), 0, 0)),
            pl.BlockSpec((1, d), lambda i, j: (0, 0)),
            pl.BlockSpec((d, tf), lambda i, j: (0, j)),
            pl.BlockSpec((d, tf), lambda i, j: (0, nf + j)),
            pl.BlockSpec((tf, d), lambda i, j: (j, 0)),
            pl.BlockSpec((1, d), lambda i, j: (0, 0)),
        ],
        out_specs=pl.BlockSpec((tm, d), lambda i, j: (i, 0)),
        out_shape=jax.ShapeDtypeStruct((t, d), F32),
        scratch_shapes=[pltpu.VMEM((tm, d), BF16), pltpu.VMEM((tm, d), F32)],
        compiler_params=pltpu.CompilerParams(
            dimension_semantics=("arbitrary", "arbitrary"), vmem_limit_bytes=VMEM_LIMIT),
        name="adaln_ffn",
    )(x, mod_l, gain.reshape(1, d), w_gu, w_gu, w_down, final_gain.reshape(1, d))


def _inproj_kernel(x_ref, mod_ref, gain_ref, cos_ref, sin_ref, win_ref, qg_ref, wuq_ref,
                   kvg_ref, wukv_ref, qk_ref, v_ref, g_ref, qm_ref, km_ref, vm_ref):
    y = _rms(x_ref[...], gain_ref[...])
    h = (y * (1 + mod_ref[4:5, :]) + mod_ref[3:4, :]).astype(BF16)
    cos = cos_ref[...]
    sin = sin_ref[...]

    def proj(c0, c1):
        return jnp.dot(h, win_ref[:, c0:c1], preferred_element_type=F32)

    half = IN_QK // 2
    for c0, mult in ((0, 1.0), (half, RET_DK ** -0.5)):
        p = proj(c0, c0 + half)
        for t in range(half // LANES):
            r = _rope(p[:, t * LANES:(t + 1) * LANES], cos, sin)
            qk_ref[:, c0 + t * LANES:c0 + (t + 1) * LANES] = (r * mult).astype(BF16)
    v_ref[...] = proj(IN_V0, IN_G0).astype(BF16)
    g_ref[...] = proj(IN_G0, IN_CQ0).astype(BF16)

    cq = _rms(proj(IN_CQ0, IN_CKV0), qg_ref[...]).astype(BF16)
    q = jnp.dot(cq, wuq_ref[...], preferred_element_type=F32)
    qscale = (MLA_NOPE + MLA_ROPE) ** -0.5
    for hd in range(MLA_HEADS):
        b0 = hd * MLA_QK_PAD
        qm_ref[:, b0:b0 + LANES] = (q[:, b0:b0 + LANES] * qscale).astype(BF16)
        r = _rope(q[:, b0 + LANES:b0 + 2 * LANES], cos, sin)
        qm_ref[:, b0 + LANES:b0 + 2 * LANES] = (r * qscale).astype(BF16)

    tail = proj(IN_CKV0, IN_COLS_PAD)
    ckv = _rms(tail[:, :MLA_KV_RANK], kvg_ref[...]).astype(BF16)
    kpe = _rope(tail[:, MLA_KV_RANK:], cos, sin).astype(BF16)
    kv = jnp.dot(ckv, wukv_ref[...], preferred_element_type=F32)
    for hd in range(MLA_HEADS):
        b0 = hd * (MLA_NOPE + MLA_DV)
        km_ref[:, hd * MLA_QK_PAD:hd * MLA_QK_PAD + LANES] = kv[:, b0:b0 + MLA_NOPE].astype(BF16)
        km_ref[:, hd * MLA_QK_PAD + LANES:(hd + 1) * MLA_QK_PAD] = kpe
        vm_ref[:, hd * MLA_DV:(hd + 1) * MLA_DV] = kv[:, b0 + MLA_NOPE:b0 + MLA_NOPE + MLA_DV].astype(BF16)


def _inproj(x, mod_l, gain, cos, sin, w_in, q_gain, w_uq, kv_gain, w_ukv, *, seq):
    t, d = x.shape
    tm = 256
    const = lambda i: (0, 0)
    row = lambda i: (i, 0)
    resident = dict(pipeline_mode=pl.Buffered(1))
    outs = [(IN_QK, BF16), (RET_WIDTH, BF16), (RET_WIDTH, BF16),
            (MLA_HEADS * MLA_QK_PAD, BF16), (MLA_HEADS * MLA_QK_PAD, BF16), (MLA_WIDTH, BF16)]
    return pl.pallas_call(
        _inproj_kernel,
        grid=(t // tm,),
        in_specs=[
            pl.BlockSpec((tm, d), row),
            pl.BlockSpec((None, N_MOD, d), lambda i: (i // (seq // tm), 0, 0)),
            pl.BlockSpec((1, d), const),
            pl.BlockSpec((tm, LANES), row),
            pl.BlockSpec((tm, LANES), row),
            pl.BlockSpec(w_in.shape, const, **resident),
            pl.BlockSpec((1, MLA_Q_RANK), const),
            pl.BlockSpec(w_uq.shape, const, **resident),
            pl.BlockSpec((1, MLA_KV_RANK), const),
            pl.BlockSpec(w_ukv.shape, const, **resident),
        ],
        out_specs=[pl.BlockSpec((tm, w), row) for w, _ in outs],
        out_shape=[jax.ShapeDtypeStruct((t, w), dt) for w, dt in outs],
        compiler_params=pltpu.CompilerParams(
            dimension_semantics=("arbitrary",), vmem_limit_bytes=VMEM_LIMIT),
        name="mixer_inproj",
    )(x, mod_l, gain.reshape(1, d), cos, sin, w_in, q_gain.reshape(1, -1), w_uq,
      kv_gain.reshape(1, -1), w_ukv)


def _retention_kernel(qk_ref, v_ref, g_ref, intra_ref, qdec_ref, kdec_ref, cdec_ref, hmask_ref,
                      gain_ref, o_ref, state_sc, *, chunks):
    @pl.when(pl.program_id(1) == 0)
    def _():
        state_sc[...] = jnp.zeros_like(state_sc)

    kbase = RET_HEADS * RET_DK

    def chunk_body(c, carry):
        rows = pl.ds(pl.multiple_of(c * RET_CHUNK, RET_CHUNK), RET_CHUNK)
        for pair in range(RET_HEADS // 2):
            qf = qk_ref[rows, pair * LANES:(pair + 1) * LANES].astype(F32)
            kp = qk_ref[rows, kbase + pair * LANES:kbase + (pair + 1) * LANES]
            kdt = (kp.astype(F32) * kdec_ref[pair]).T.astype(BF16)
            for hh in range(2):
                hd = 2 * pair + hh
                cols = slice(hd * RET_DV, (hd + 1) * RET_DV)
                vh = v_ref[rows, cols]
                qm = (qf * hmask_ref[hh]).astype(BF16)
                s = lax.dot_general(qm, kp, (((1,), (1,)), ((), ())), preferred_element_type=F32)
                intra = jnp.dot((s * intra_ref[hd]).astype(BF16), vh, preferred_element_type=F32)
                qd = (qf * qdec_ref[hd]).astype(BF16)
                state = state_sc[hd]
                cross = jnp.dot(qd, state.astype(BF16), preferred_element_type=F32)
                state_sc[hd] = state * cdec_ref[hd] + jnp.dot(kdt, vh, preferred_element_type=F32)
                yh = _rms(intra + cross, gain_ref[:, cols])
                o_ref[rows, cols] = (jax.nn.silu(g_ref[rows, cols].astype(F32)) * yh).astype(BF16)
        return carry

    lax.fori_loop(0, chunks, chunk_body, 0)


def _retention_consts():
    c = RET_CHUNK
    log_g = jnp.log1p(-(2.0 ** (-5.0 - jnp.arange(RET_HEADS, dtype=F32))))
    idx = jnp.arange(c, dtype=F32)
    rel = idx[:, None] - idx[None, :]
    intra = jnp.where(rel >= 0, jnp.exp(log_g[:, None, None] * jnp.maximum(rel, 0.0)), 0.0)
    q_decay = jnp.exp(log_g[:, None] * (idx + 1.0))
    k_decay = jnp.exp(log_g[:, None] * (c - 1.0 - idx))
    chunk_decay = jnp.exp(log_g * c)
    own = (np.arange(LANES)[None, :] // RET_DK == np.arange(2)[:, None]).astype(np.float32)
    hmask = jnp.asarray(own).reshape(2, 1, LANES)
    qdec = q_decay[:, :, None] * jnp.asarray(own)[jnp.arange(RET_HEADS) % 2][:, None, :]
    kdec = jnp.repeat(k_decay.reshape(RET_HEADS // 2, 2, c), RET_DK, axis=1).transpose(0, 2, 1)
    cdec = jnp.broadcast_to(chunk_decay[:, None, None], (RET_HEADS, 1, LANES))
    return intra, qdec, kdec, cdec, hmask


def _retention(qk, v, g, gain, *, batch, seq):
    tb = 1024
    chunks = tb // RET_CHUNK
    nb = seq // tb
    intra, qdec, kdec, cdec, hmask = _retention_consts()
    row = lambda b, i: (b * nb + i, 0)
    c3 = lambda b, i: (0, 0, 0)
    return pl.pallas_call(
        functools.partial(_retention_kernel, chunks=chunks),
        grid=(batch, nb),
        in_specs=[
            pl.BlockSpec((tb, IN_QK), row),
            pl.BlockSpec((tb, RET_WIDTH), row),
            pl.BlockSpec((tb, RET_WIDTH), row),
            pl.BlockSpec(intra.shape, c3),
            pl.BlockSpec(qdec.shape, c3),
            pl.BlockSpec(kdec.shape, c3),
            pl.BlockSpec(cdec.shape, c3),
            pl.BlockSpec(hmask.shape, c3),
            pl.BlockSpec((1, RET_WIDTH), lambda b, i: (0, 0)),
        ],
        out_specs=pl.BlockSpec((tb, RET_WIDTH), row),
        out_shape=jax.ShapeDtypeStruct((batch * seq, RET_WIDTH), BF16),
        scratch_shapes=[pltpu.VMEM((RET_HEADS, LANES, RET_DV), F32)],
        compiler_params=pltpu.CompilerParams(
            dimension_semantics=("arbitrary", "arbitrary"), vmem_limit_bytes=VMEM_LIMIT),
        name="retention",
    )(qk, v, g, intra, qdec, kdec, cdec, hmask, gain.reshape(1, RET_WIDTH))


def _attn_kernel(q_ref, k_ref, v_ref, o_ref, m_sc, l_sc, acc_sc, *, blk):
    qi = pl.program_id(2)
    q = q_ref[...]
    m_sc[...] = jnp.full_like(m_sc, -jnp.inf)
    l_sc[...] = jnp.zeros_like(l_sc)
    acc_sc[...] = jnp.zeros_like(acc_sc)

    def step(ki, masked):
        rows = pl.ds(pl.multiple_of(ki * blk, blk), blk)
        s = lax.dot_general(q, k_ref[rows, :], (((1,), (1,)), ((), ())), preferred_element_type=F32)
        if masked:
            qpos = lax.broadcasted_iota(jnp.int32, s.shape, 0)
            kpos = lax.broadcasted_iota(jnp.int32, s.shape, 1)
            s = jnp.where(kpos <= qpos, s, -jnp.inf)
        m_prev = m_sc[...]
        m_new = jnp.maximum(m_prev, jnp.max(s, axis=-1, keepdims=True))
        p = jnp.exp(s - m_new)
        alpha = jnp.exp(m_prev - m_new)
        l_sc[...] = alpha * l_sc[...] + jnp.sum(p, axis=-1, keepdims=True)
        acc_sc[...] = alpha * acc_sc[...] + jnp.dot(p.astype(BF16), v_ref[rows, :],
                                                    preferred_element_type=F32)
        m_sc[...] = m_new

    def full_block(ki, carry):
        step(ki, False)
        return carry

    lax.fori_loop(0, qi, full_block, 0)
    step(qi, True)
    o_ref[...] = (acc_sc[...] / l_sc[...]).astype(BF16)


def _attention(qm, km, vm, *, batch, seq):
    blk = 512
    nq = seq // blk
    qm = qm.reshape(batch, seq, -1)
    km = km.reshape(batch, seq, -1)
    vm = vm.reshape(batch, seq, -1)
    out = pl.pallas_call(
        functools.partial(_attn_kernel, blk=blk),
        grid=(batch, MLA_HEADS, nq),
        in_specs=[
            pl.BlockSpec((None, blk, MLA_QK_PAD), lambda b, h, i: (b, i, h)),
            pl.BlockSpec((None, seq, MLA_QK_PAD), lambda b, h, i: (b, 0, h)),
            pl.BlockSpec((None, seq, MLA_DV), lambda b, h, i: (b, 0, h)),
        ],
        out_specs=pl.BlockSpec((None, blk, MLA_DV), lambda b, h, i: (b, i, h)),
        out_shape=jax.ShapeDtypeStruct((batch, seq, MLA_WIDTH), BF16),
        scratch_shapes=[pltpu.VMEM((blk, 1), F32), pltpu.VMEM((blk, 1), F32),
                        pltpu.VMEM((blk, MLA_DV), F32)],
        compiler_params=pltpu.CompilerParams(
            dimension_semantics=("arbitrary", "arbitrary", "arbitrary"),
            vmem_limit_bytes=VMEM_LIMIT),
        name="mla_attention",
    )(qm, km, vm)
    return out.reshape(batch * seq, MLA_WIDTH)


def _outproj_kernel(x_ref, mod_ref, yr_ref, ym_ref, w_ref, o_ref):
    y = jnp.dot(yr_ref[...], w_ref[:RET_WIDTH, :], preferred_element_type=F32)
    y += jnp.dot(ym_ref[...], w_ref[RET_WIDTH:, :], preferred_element_type=F32)
    o_ref[...] = x_ref[...] + mod_ref[5:6, :] * y


def _outproj(x, mod_l, y_r, y_m, w_out, *, seq):
    t, d = x.shape
    tm = 512
    row = lambda i: (i, 0)
    return pl.pallas_call(
        _outproj_kernel,
        grid=(t // tm,),
        in_specs=[
            pl.BlockSpec((tm, d), row),
            pl.BlockSpec((None, N_MOD, d), lambda i: (i // (seq // tm), 0, 0)),
            pl.BlockSpec((tm, RET_WIDTH), row),
            pl.BlockSpec((tm, MLA_WIDTH), row),
            pl.BlockSpec(w_out.shape, lambda i: (0, 0), pipeline_mode=pl.Buffered(1)),
        ],
        out_specs=pl.BlockSpec((tm, d), row),
        out_shape=jax.ShapeDtypeStruct((t, d), F32),
        compiler_params=pltpu.CompilerParams(
            dimension_semantics=("arbitrary",), vmem_limit_bytes=VMEM_LIMIT),
        name="mixer_outproj",
    )(x, mod_l, y_r, y_m, w_out)


def kernel(x, c, positions, w_ada, b_ada, norm_ffn1, ffn1_w_gu, ffn1_w_down, norm_mix, w_in,
           ret_norm_g, q_norm_g, w_uq, kv_norm_g, w_ukv, w_out, norm_ffn2, ffn2_w_gu,
           ffn2_w_down, final_norm):
    batch, seq, d = x.shape
    depth = w_ada.shape[0]
    xt = x.reshape(batch * seq, d)

    c_pad = jnp.pad(c, ((0, 8 - batch), (0, 0)))
    mod = _adaln_mod(c_pad, w_ada, b_ada)[:, :batch].reshape(depth, batch, N_MOD, d)
    cos, sin = _rope_tables(positions)

    w_in_p = jnp.pad(w_in, ((0, 0), (0, 0), (0, IN_COLS_PAD - w_in.shape[-1]))).astype(BF16)
    w_uq_p = jnp.pad(w_uq.reshape(depth, MLA_Q_RANK, MLA_HEADS, MLA_NOPE + MLA_ROPE),
                     ((0, 0), (0, 0), (0, 0), (0, MLA_QK_PAD - MLA_NOPE - MLA_ROPE))
                     ).reshape(depth, MLA_Q_RANK, MLA_HEADS * MLA_QK_PAD).astype(BF16)

    for l in range(depth):
        xt = _ffn(xt, mod[l], norm_ffn1[l], ffn1_w_gu[l].astype(BF16), ffn1_w_down[l].astype(BF16),
                  final_norm, mod_base=0, final_norm=False, seq=seq)
        qk, v_r, g_r, qm, km, vm = _inproj(
            xt, mod[l], norm_mix[l], cos, sin, w_in_p[l], q_norm_g[l], w_uq_p[l], kv_norm_g[l],
            w_ukv[l].astype(BF16), seq=seq)
        y_r = _retention(qk, v_r, g_r, ret_norm_g[l], batch=batch, seq=seq)
        y_m = _attention(qm, km, vm, batch=batch, seq=seq)
        xt = _outproj(xt, mod[l], y_r, y_m, w_out[l].astype(BF16), seq=seq)
        xt = _ffn(xt, mod[l], norm_ffn2[l], ffn2_w_gu[l].astype(BF16), ffn2_w_down[l].astype(BF16),
                  final_norm, mod_base=6, final_norm=(l == depth - 1), seq=seq)
    return xt.reshape(batch, seq, d)
```

```python
import functools

import jax
import jax.numpy as jnp
import numpy as np
from jax import lax
from jax.experimental import pallas as pl
from jax.experimental.pallas import tpu as pltpu

D_MODEL = 2048
DEPTH = 2
RET_HEADS = 8
RET_DK = 64
RET_DV = 128
RET_CHUNK = 128
MLA_HEADS = 8
MLA_Q_RANK = 512
MLA_KV_RANK = 256
MLA_NOPE = 128
MLA_ROPE = 64
MLA_DV = 128
RET_WIDTH = RET_HEADS * RET_DV
MLA_WIDTH = MLA_HEADS * MLA_DV
ROPE_DIM = 64
ROPE_BASE = 10000.0
EPS = 1e-6
N_MOD = 9

LANES = 128
MLA_QK_PAD = 256
IN_QK = 2 * RET_HEADS * RET_DK
IN_V0 = IN_QK
IN_G0 = IN_V0 + RET_WIDTH
IN_CQ0 = IN_G0 + RET_WIDTH
IN_CKV0 = IN_CQ0 + MLA_Q_RANK
IN_KPE0 = IN_CKV0 + MLA_KV_RANK
IN_COLS_PAD = IN_KPE0 + LANES

VMEM_LIMIT = 56 * 1024 * 1024
ATTN_BLK = 512
ATTN_HEADS = 2
LOG2_E = 1.4426950408889634

BF16 = jnp.bfloat16
F32 = jnp.float32


def _rms(xf, gain):
    return xf * lax.rsqrt(jnp.mean(xf * xf, axis=-1, keepdims=True) + EPS) * gain


def _rope(blk, cos, sin_signed):
    lane = lax.broadcasted_iota(jnp.int32, blk.shape, 1)
    partner = jnp.where(lane % ROPE_DIM < ROPE_DIM // 2,
                        pltpu.roll(blk, LANES - ROPE_DIM // 2, 1),
                        pltpu.roll(blk, ROPE_DIM // 2, 1))
    return blk * cos + partner * sin_signed


def _mod_kernel(c_ref, w_ref, b_ref, o_ref):
    ca = jax.nn.silu(c_ref[...]).astype(BF16)
    o_ref[...] = jnp.dot(ca, w_ref[...].astype(BF16), preferred_element_type=F32) + b_ref[...]


def _adaln_mod(c_pad, w_ada, b_ada):
    depth, d, n = w_ada.shape
    rows = c_pad.shape[0]
    tn = 1024
    return pl.pallas_call(
        _mod_kernel,
        grid=(depth, n // tn),
        in_specs=[
            pl.BlockSpec((rows, d), lambda l, j: (0, 0)),
            pl.BlockSpec((None, d, tn), lambda l, j: (l, 0, j)),
            pl.BlockSpec((None, 1, tn), lambda l, j: (l, 0, j)),
        ],
        out_specs=pl.BlockSpec((None, rows, tn), lambda l, j: (l, 0, j)),
        out_shape=jax.ShapeDtypeStruct((depth, rows, n), F32),
        compiler_params=pltpu.CompilerParams(
            dimension_semantics=("arbitrary", "arbitrary"), vmem_limit_bytes=VMEM_LIMIT),
        name="adaln_mod",
    )(c_pad, w_ada, b_ada.reshape(depth, 1, n))


def _rope_table_kernel(pos_ref, inv_ref, sign_ref, cos_ref, sin_ref):
    ang = pos_ref[...].astype(F32) * inv_ref[...]
    cos_ref[...] = jnp.cos(ang)
    sin_ref[...] = jnp.sin(ang) * sign_ref[...]


def _rope_tables(positions):
    t = positions.size
    tm = 2048
    inv = ROPE_BASE ** (-jnp.arange(0, ROPE_DIM, 2, dtype=F32) / ROPE_DIM)
    inv_l = jnp.tile(inv, LANES // (ROPE_DIM // 2)).reshape(1, LANES)
    sign = np.where(np.arange(LANES) % ROPE_DIM < ROPE_DIM // 2, -1.0, 1.0).astype(np.float32)
    return pl.pallas_call(
        _rope_table_kernel,
        grid=(t // tm,),
        in_specs=[
            pl.BlockSpec((tm, 1), lambda i: (i, 0)),
            pl.BlockSpec((1, LANES), lambda i: (0, 0)),
            pl.BlockSpec((1, LANES), lambda i: (0, 0)),
        ],
        out_specs=[pl.BlockSpec((tm, LANES), lambda i: (i, 0))] * 2,
        out_shape=[jax.ShapeDtypeStruct((t, LANES), F32)] * 2,
        compiler_params=pltpu.CompilerParams(dimension_semantics=("arbitrary",)),
        name="rope_tables",
    )(positions.reshape(t, 1), inv_l, jnp.asarray(sign).reshape(1, LANES))


def _ffn_kernel(x_ref, mod_ref, gain_ref, wg_ref, wu_ref, wd_ref, fgain_ref, o_ref,
                h_sc, acc_sc, *, mod_base, final_norm):
    j = pl.program_id(1)

    @pl.when(j == 0)
    def _():
        y = _rms(x_ref[...], gain_ref[...])
        shift = mod_ref[mod_base:mod_base + 1, :]
        scale = mod_ref[mod_base + 1:mod_base + 2, :]
        h_sc[...] = (y * (1 + scale) + shift).astype(BF16)
        acc_sc[...] = jnp.zeros_like(acc_sc)

    h = h_sc[...]
    g = jnp.dot(h, wg_ref[...], preferred_element_type=F32)
    u = jnp.dot(h, wu_ref[...], preferred_element_type=F32)
    a = (jax.nn.silu(g) * u).astype(BF16)
    acc_sc[...] += jnp.dot(a, wd_ref[...], preferred_element_type=F32)

    @pl.when(j == pl.num_programs(1) - 1)
    def _():
        gate = mod_ref[mod_base + 2:mod_base + 3, :]
        out = x_ref[...] + 0.5 * gate * acc_sc[...]
        if final_norm:
            out = _rms(out, fgain_ref[...])
        o_ref[...] = out


def _ffn(x, mod_l, gain, w_gu, w_down, final_gain, *, mod_base, final_norm, seq):
    t, d = x.shape
    f = w_down.shape[0]
    tm, tf = 512, 512
    nf = f // tf
    kern = functools.partial(_ffn_kernel, mod_base=mod_base, final_norm=final_norm)
    return pl.pallas_call(
        kern,
        grid=(t // tm, nf),
        in_specs=[
            pl.BlockSpec((tm, d), lambda i, j: (i, 0)),
            pl.BlockSpec((None, N_MOD, d), lambda i, j: (i // (seq // tm), 0, 0)),
            pl.BlockSpec((1, d), lambda i, j: (0, 0)),
            pl.BlockSpec((d, tf), lambda i, j: (0, j)),
            pl.BlockSpec((d, tf), lambda i, j: (0, nf + j)),
            pl.BlockSpec((tf, d), lambda i, j: (j, 0)),
            pl.BlockSpec((1, d), lambda i, j: (0, 0)),
        ],
        out_specs=pl.BlockSpec((tm, d), lambda i, j: (i, 0)),
        out_shape=jax.ShapeDtypeStruct((t, d), F32),
        scratch_shapes=[pltpu.VMEM((tm, d), BF16), pltpu.VMEM((tm, d), F32)],
        compiler_params=pltpu.CompilerParams(
            dimension_semantics=("arbitrary", "arbitrary"), vmem_limit_bytes=VMEM_LIMIT),
        name="adaln_ffn",
    )(x, mod_l, gain.reshape(1, d), w_gu, w_gu, w_down, final_gain.reshape(1, d))


def _inproj_kernel(x_ref, mod_ref, gain_ref, cos_ref, sin_ref, win_ref, qg_ref, wuq_ref,
                   kvg_ref, wukv_ref, qk_ref, v_ref, g_ref, qm_ref, km_ref, vm_ref):
    y = _rms(x_ref[...], gain_ref[...])
    h = (y * (1 + mod_ref[4:5, :]) + mod_ref[3:4, :]).astype(BF16)
    cos = cos_ref[...]
    sin = sin_ref[...]

    def proj(c0, c1):
        return jnp.dot(h, win_ref[:, c0:c1], preferred_element_type=F32)

    half = IN_QK // 2
    for c0, mult in ((0, 1.0), (half, RET_DK ** -0.5)):
        p = proj(c0, c0 + half)
        for t in range(half // LANES):
            r = _rope(p[:, t * LANES:(t + 1) * LANES], cos, sin)
            qk_ref[:, c0 + t * LANES:c0 + (t + 1) * LANES] = (r * mult).astype(BF16)
    v_ref[...] = proj(IN_V0, IN_G0).astype(BF16)
    g_ref[...] = proj(IN_G0, IN_CQ0).astype(BF16)

    cq = _rms(proj(IN_CQ0, IN_CKV0), qg_ref[...]).astype(BF16)
    q = jnp.dot(cq, wuq_ref[...], preferred_element_type=F32)
    qscale = (MLA_NOPE + MLA_ROPE) ** -0.5 * LOG2_E
    for hd in range(MLA_HEADS):
        b0 = hd * MLA_QK_PAD
        qm_ref[:, b0:b0 + LANES] = (q[:, b0:b0 + LANES] * qscale).astype(BF16)
        r = _rope(q[:, b0 + LANES:b0 + 2 * LANES], cos, sin)
        qm_ref[:, b0 + LANES:b0 + 2 * LANES] = (r * qscale).astype(BF16)

    tail = proj(IN_CKV0, IN_COLS_PAD)
    ckv = _rms(tail[:, :MLA_KV_RANK], kvg_ref[...]).astype(BF16)
    kpe = _rope(tail[:, MLA_KV_RANK:], cos, sin).astype(BF16)
    kv = jnp.dot(ckv, wukv_ref[...], preferred_element_type=F32)
    for hd in range(MLA_HEADS):
        b0 = hd * (MLA_NOPE + MLA_DV)
        km_ref[:, hd * MLA_QK_PAD:hd * MLA_QK_PAD + LANES] = kv[:, b0:b0 + MLA_NOPE].astype(BF16)
        km_ref[:, hd * MLA_QK_PAD + LANES:(hd + 1) * MLA_QK_PAD] = kpe
        vm_ref[hd * MLA_DV:(hd + 1) * MLA_DV, :] = (
            kv[:, b0 + MLA_NOPE:b0 + MLA_NOPE + MLA_DV].T.astype(BF16))


def _inproj(x, mod_l, gain, cos, sin, w_in, q_gain, w_uq, kv_gain, w_ukv, *, seq):
    t, d = x.shape
    tm = 256
    const = lambda i: (0, 0)
    row = lambda i: (i, 0)
    resident = dict(pipeline_mode=pl.Buffered(1))
    widths = [IN_QK, RET_WIDTH, RET_WIDTH, MLA_HEADS * MLA_QK_PAD, MLA_HEADS * MLA_QK_PAD]
    tpb = seq // tm
    sub = ATTN_BLK // tm
    vt_spec = pl.BlockSpec((None, None, MLA_WIDTH, tm),
                           lambda i: (i // tpb, (i % tpb) // sub, 0, i % sub))
    vt_shape = jax.ShapeDtypeStruct((t // seq, seq // ATTN_BLK, MLA_WIDTH, ATTN_BLK), BF16)
    return pl.pallas_call(
        _inproj_kernel,
        grid=(t // tm,),
        in_specs=[
            pl.BlockSpec((tm, d), row),
            pl.BlockSpec((None, N_MOD, d), lambda i: (i // (seq // tm), 0, 0)),
            pl.BlockSpec((1, d), const),
            pl.BlockSpec((tm, LANES), row),
            pl.BlockSpec((tm, LANES), row),
            pl.BlockSpec(w_in.shape, const, **resident),
            pl.BlockSpec((1, MLA_Q_RANK), const),
            pl.BlockSpec(w_uq.shape, const, **resident),
            pl.BlockSpec((1, MLA_KV_RANK), const),
            pl.BlockSpec(w_ukv.shape, const, **resident),
        ],
        out_specs=[pl.BlockSpec((tm, w), row) for w in widths] + [vt_spec],
        out_shape=[jax.ShapeDtypeStruct((t, w), BF16) for w in widths] + [vt_shape],
        compiler_params=pltpu.CompilerParams(
            dimension_semantics=("arbitrary",), vmem_limit_bytes=VMEM_LIMIT),
        name="mixer_inproj",
    )(x, mod_l, gain.reshape(1, d), cos, sin, w_in, q_gain.reshape(1, -1), w_uq,
      kv_gain.reshape(1, -1), w_ukv)


def _retention_kernel(qk_ref, v_ref, g_ref, intra_ref, qdec_ref, kdec_ref, cdec_ref, hmask_ref,
                      gain_ref, o_ref, state_sc, *, chunks):
    @pl.when(pl.program_id(1) == 0)
    def _():
        state_sc[...] = jnp.zeros_like(state_sc)

    kbase = RET_HEADS * RET_DK

    def chunk_body(c, carry):
        rows = pl.ds(pl.multiple_of(c * RET_CHUNK, RET_CHUNK), RET_CHUNK)
        for pair in range(RET_HEADS // 2):
            qf = qk_ref[rows, pair * LANES:(pair + 1) * LANES].astype(F32)
            kp = qk_ref[rows, kbase + pair * LANES:kbase + (pair + 1) * LANES]
            kdt = (kp.astype(F32) * kdec_ref[pair]).T.astype(BF16)
            for hh in range(2):
                hd = 2 * pair + hh
                cols = slice(hd * RET_DV, (hd + 1) * RET_DV)
                vh = v_ref[rows, cols]
                qm = (qf * hmask_ref[hh]).astype(BF16)
                s = lax.dot_general(qm, kp, (((1,), (1,)), ((), ())), preferred_element_type=F32)
                intra = jnp.dot((s * intra_ref[hd]).astype(BF16), vh, preferred_element_type=F32)
                qd = (qf * qdec_ref[hd]).astype(BF16)
                state = state_sc[hd]
                cross = jnp.dot(qd, state.astype(BF16), preferred_element_type=F32)
                state_sc[hd] = state * cdec_ref[hd] + jnp.dot(kdt, vh, preferred_element_type=F32)
                yh = _rms(intra + cross, gain_ref[:, cols])
                o_ref[rows, cols] = (jax.nn.silu(g_ref[rows, cols].astype(F32)) * yh).astype(BF16)
        return carry

    lax.fori_loop(0, chunks, chunk_body, 0)


def _retention_consts():
    c = RET_CHUNK
    log_g = jnp.log1p(-(2.0 ** (-5.0 - jnp.arange(RET_HEADS, dtype=F32))))
    idx = jnp.arange(c, dtype=F32)
    rel = idx[:, None] - idx[None, :]
    intra = jnp.where(rel >= 0, jnp.exp(log_g[:, None, None] * jnp.maximum(rel, 0.0)), 0.0)
    q_decay = jnp.exp(log_g[:, None] * (idx + 1.0))
    k_decay = jnp.exp(log_g[:, None] * (c - 1.0 - idx))
    chunk_decay = jnp.exp(log_g * c)
    own = (np.arange(LANES)[None, :] // RET_DK == np.arange(2)[:, None]).astype(np.float32)
    hmask = jnp.asarray(own).reshape(2, 1, LANES)
    qdec = q_decay[:, :, None] * jnp.asarray(own)[jnp.arange(RET_HEADS) % 2][:, None, :]
    kdec = jnp.repeat(k_decay.reshape(RET_HEADS // 2, 2, c), RET_DK, axis=1).transpose(0, 2, 1)
    cdec = jnp.broadcast_to(chunk_decay[:, None, None], (RET_HEADS, 1, LANES))
    return intra, qdec, kdec, cdec, hmask


def _retention(qk, v, g, gain, *, batch, seq):
    tb = 1024
    chunks = tb // RET_CHUNK
    nb = seq // tb
    intra, qdec, kdec, cdec, hmask = _retention_consts()
    row = lambda b, i: (b * nb + i, 0)
    c3 = lambda b, i: (0, 0, 0)
    return pl.pallas_call(
        functools.partial(_retention_kernel, chunks=chunks),
        grid=(batch, nb),
        in_specs=[
            pl.BlockSpec((tb, IN_QK), row),
            pl.BlockSpec((tb, RET_WIDTH), row),
            pl.BlockSpec((tb, RET_WIDTH), row),
            pl.BlockSpec(intra.shape, c3),
            pl.BlockSpec(qdec.shape, c3),
            pl.BlockSpec(kdec.shape, c3),
            pl.BlockSpec(cdec.shape, c3),
            pl.BlockSpec(hmask.shape, c3),
            pl.BlockSpec((1, RET_WIDTH), lambda b, i: (0, 0)),
        ],
        out_specs=pl.BlockSpec((tb, RET_WIDTH), row),
        out_shape=jax.ShapeDtypeStruct((batch * seq, RET_WIDTH), BF16),
        scratch_shapes=[pltpu.VMEM((RET_HEADS, LANES, RET_DV), F32)],
        compiler_params=pltpu.CompilerParams(
            dimension_semantics=("arbitrary", "arbitrary"), vmem_limit_bytes=VMEM_LIMIT),
        name="retention",
    )(qk, v, g, intra, qdec, kdec, cdec, hmask, gain.reshape(1, RET_WIDTH))


def _attn_kernel(q_ref, k_ref, vt_ref, o_ref, s_sc, acc_sc, m_sc, l_sc):
    qi = pl.program_id(2)
    blk = ATTN_BLK
    acc_sc[...] = jnp.zeros_like(acc_sc)

    def scores(ki, slot):
        rows = pl.ds(pl.multiple_of(ki * blk, blk), blk)
        for hd in range(ATTN_HEADS):
            cols = slice(hd * MLA_QK_PAD, (hd + 1) * MLA_QK_PAD)
            s_sc[slot, hd] = lax.dot_general(k_ref[rows, cols], q_ref[:, cols],
                                             (((1,), (1,)), ((), ())), preferred_element_type=F32)

    def update(ki, slot, stats, masked):
        new_stats = []
        for hd in range(ATTN_HEADS):
            m_prev, l_prev = stats[hd]
            s = s_sc[slot, hd]
            if masked:
                kpos = lax.broadcasted_iota(jnp.int32, s.shape, 0)
                qpos = lax.broadcasted_iota(jnp.int32, s.shape, 1)
                s = jnp.where(kpos <= qpos, s, -jnp.inf)
            m_new = jnp.maximum(m_prev, jnp.max(s, axis=0, keepdims=True))
            p = jnp.exp2(s - m_new)
            alpha = jnp.exp2(m_prev - m_new)
            l_new = alpha * l_prev + jnp.sum(p, axis=0, keepdims=True)
            vt = vt_ref[ki, hd * MLA_DV:(hd + 1) * MLA_DV, :]
            acc_sc[hd] = alpha * acc_sc[hd] + jnp.dot(vt, p.astype(BF16), preferred_element_type=F32)
            new_stats.append((m_new, l_new))
        return tuple(new_stats)

    def load_stats():
        return tuple((m_sc[hd], l_sc[hd]) for hd in range(ATTN_HEADS))

    def store_stats(stats):
        for hd in range(ATTN_HEADS):
            m_sc[hd], l_sc[hd] = stats[hd]

    scores(0, 0)

    def block_pair(kp, stats):
        scores(2 * kp + 1, 1)
        stats = update(2 * kp, 0, stats, False)
        scores(2 * kp + 2, 0)
        stats = update(2 * kp + 1, 1, stats, False)
        return stats

    init = tuple((jnp.full((1, blk), -jnp.inf, F32), jnp.zeros((1, blk), F32))
                 for _ in range(ATTN_HEADS))
    store_stats(lax.fori_loop(0, qi // 2, block_pair, init))

    @pl.when(qi % 2 == 1)
    def _():
        stats = update(qi - 1, 0, load_stats(), False)
        scores(qi, 1)
        store_stats(stats)

    stats = update(qi, qi % 2, load_stats(), True)
    for hd in range(ATTN_HEADS):
        out_t = acc_sc[hd] / stats[hd][1]
        o_ref[:, hd * MLA_DV:(hd + 1) * MLA_DV] = out_t.T.astype(BF16)


def _attention(qm, km, vt, *, batch, seq):
    blk = ATTN_BLK
    nq = seq // blk
    qm = qm.reshape(batch, seq, -1)
    km = km.reshape(batch, seq, -1)
    out = pl.pallas_call(
        _attn_kernel,
        grid=(batch, MLA_HEADS // ATTN_HEADS, nq),
        in_specs=[
            pl.BlockSpec((None, blk, ATTN_HEADS * MLA_QK_PAD), lambda b, h, i: (b, i, h)),
            pl.BlockSpec((None, seq, ATTN_HEADS * MLA_QK_PAD), lambda b, h, i: (b, 0, h)),
            pl.BlockSpec((None, nq, ATTN_HEADS * MLA_DV, blk), lambda b, h, i: (b, 0, h, 0)),
        ],
        out_specs=pl.BlockSpec((None, blk, ATTN_HEADS * MLA_DV), lambda b, h, i: (b, i, h)),
        out_shape=jax.ShapeDtypeStruct((batch, seq, MLA_WIDTH), BF16),
        scratch_shapes=[pltpu.VMEM((2, ATTN_HEADS, blk, blk), F32),
                        pltpu.VMEM((ATTN_HEADS, MLA_DV, blk), F32),
                        pltpu.VMEM((ATTN_HEADS, 1, blk), F32), pltpu.VMEM((ATTN_HEADS, 1, blk), F32)],
        compiler_params=pltpu.CompilerParams(
            dimension_semantics=("arbitrary", "arbitrary", "arbitrary"),
            vmem_limit_bytes=VMEM_LIMIT),
        name="mla_attention",
    )(qm, km, vt)
    return out.reshape(batch * seq, MLA_WIDTH)


def _outproj_kernel(x_ref, mod_ref, yr_ref, ym_ref, w_ref, o_ref):
    y = jnp.dot(yr_ref[...], w_ref[:RET_WIDTH, :], preferred_element_type=F32)
    y += jnp.dot(ym_ref[...], w_ref[RET_WIDTH:, :], preferred_element_type=F32)
    o_ref[...] = x_ref[...] + mod_ref[5:6, :] * y


def _outproj(x, mod_l, y_r, y_m, w_out, *, seq):
    t, d = x.shape
    tm = 512
    row = lambda i: (i, 0)
    return pl.pallas_call(
        _outproj_kernel,
        grid=(t // tm,),
        in_specs=[
            pl.BlockSpec((tm, d), row),
            pl.BlockSpec((None, N_MOD, d), lambda i: (i // (seq // tm), 0, 0)),
            pl.BlockSpec((tm, RET_WIDTH), row),
            pl.BlockSpec((tm, MLA_WIDTH), row),
            pl.BlockSpec(w_out.shape, lambda i: (0, 0), pipeline_mode=pl.Buffered(1)),
        ],
        out_specs=pl.BlockSpec((tm, d), row),
        out_shape=jax.ShapeDtypeStruct((t, d), F32),
        compiler_params=pltpu.CompilerParams(
            dimension_semantics=("arbitrary",), vmem_limit_bytes=VMEM_LIMIT),
        name="mixer_outproj",
    )(x, mod_l, y_r, y_m, w_out)


def kernel(x, c, positions, w_ada, b_ada, norm_ffn1, ffn1_w_gu, ffn1_w_down, norm_mix, w_in,
           ret_norm_g, q_norm_g, w_uq, kv_norm_g, w_ukv, w_out, norm_ffn2, ffn2_w_gu,
           ffn2_w_down, final_norm):
    batch, seq, d = x.shape
    depth = w_ada.shape[0]
    xt = x.reshape(batch * seq, d)

    c_pad = jnp.pad(c, ((0, 8 - batch), (0, 0)))
    mod = _adaln_mod(c_pad, w_ada, b_ada)[:, :batch].reshape(depth, batch, N_MOD, d)
    cos, sin = _rope_tables(positions)

    w_in_p = jnp.pad(w_in, ((0, 0), (0, 0), (0, IN_COLS_PAD - w_in.shape[-1]))).astype(BF16)
    w_uq_p = jnp.pad(w_uq.reshape(depth, MLA_Q_RANK, MLA_HEADS, MLA_NOPE + MLA_ROPE),
                     ((0, 0), (0, 0), (0, 0), (0, MLA_QK_PAD - MLA_NOPE - MLA_ROPE))
                     ).reshape(depth, MLA_Q_RANK, MLA_HEADS * MLA_QK_PAD).astype(BF16)

    for l in range(depth):
        xt = _ffn(xt, mod[l], norm_ffn1[l], ffn1_w_gu[l].astype(BF16), ffn1_w_down[l].astype(BF16),
                  final_norm, mod_base=0, final_norm=False, seq=seq)
        qk, v_r, g_r, qm, km, vm = _inproj(
            xt, mod[l], norm_mix[l], cos, sin, w_in_p[l], q_norm_g[l], w_uq_p[l], kv_norm_g[l],
            w_ukv[l].astype(BF16), seq=seq)
        y_r = _retention(qk, v_r, g_r, ret_norm_g[l], batch=batch, seq=seq)
        y_m = _attention(qm, km, vm, batch=batch, seq=seq)
        xt = _outproj(xt, mod[l], y_r, y_m, w_out[l].astype(BF16), seq=seq)
        xt = _ffn(xt, mod[l], norm_ffn2[l], ffn2_w_gu[l].astype(BF16), ffn2_w_down[l].astype(BF16),
                  final_norm, mod_base=6, final_norm=(l == depth - 1), seq=seq)
    return xt.reshape(batch, seq, d)
```

```python
import functools

import jax
import jax.numpy as jnp
import numpy as np
from jax import lax
from jax.experimental import pallas as pl
from jax.experimental.pallas import tpu as pltpu

D_MODEL = 2048
DEPTH = 2
RET_HEADS = 8
RET_DK = 64
RET_DV = 128
RET_CHUNK = 128
MLA_HEADS = 8
MLA_Q_RANK = 512
MLA_KV_RANK = 256
MLA_NOPE = 128
MLA_ROPE = 64
MLA_DV = 128
RET_WIDTH = RET_HEADS * RET_DV
MLA_WIDTH = MLA_HEADS * MLA_DV
ROPE_DIM = 64
ROPE_BASE = 10000.0
EPS = 1e-6
N_MOD = 9

LANES = 128
MLA_QK_PAD = 256
IN_QK = 2 * RET_HEADS * RET_DK
IN_V0 = IN_QK
IN_G0 = IN_V0 + RET_WIDTH
IN_CQ0 = IN_G0 + RET_WIDTH
IN_CKV0 = IN_CQ0 + MLA_Q_RANK
IN_KPE0 = IN_CKV0 + MLA_KV_RANK
IN_COLS_PAD = IN_KPE0 + LANES

VMEM_LIMIT = 56 * 1024 * 1024
ATTN_BLK = 512
ATTN_HEADS = 2
LOG2_E = 1.4426950408889634
CAST_BLOCK_BYTES = 8 * 1024 * 1024

BF16 = jnp.bfloat16
F32 = jnp.float32


def _rms(xf, gain):
    return xf * lax.rsqrt(jnp.mean(xf * xf, axis=-1, keepdims=True) + EPS) * gain


def _rope(blk, cos, sin_signed):
    lane = lax.broadcasted_iota(jnp.int32, blk.shape, 1)
    partner = jnp.where(lane % ROPE_DIM < ROPE_DIM // 2,
                        pltpu.roll(blk, LANES - ROPE_DIM // 2, 1),
                        pltpu.roll(blk, ROPE_DIM // 2, 1))
    return blk * cos + partner * sin_signed


def _cast_kernel(w_ref, o_ref):
    cols = w_ref.shape[-1]
    o_ref[:, :cols] = w_ref[...].astype(BF16)
    if o_ref.shape[-1] > cols:
        o_ref[:, cols:] = jnp.zeros((o_ref.shape[0], o_ref.shape[-1] - cols), BF16)


def _to_bf16(w, out_cols=None):
    depth, rows, cols = w.shape
    out_cols = out_cols or cols
    tr = 16
    while rows % (2 * tr) == 0 and 2 * tr * cols * 4 <= CAST_BLOCK_BYTES:
        tr *= 2
    return pl.pallas_call(
        _cast_kernel,
        grid=(depth, rows // tr),
        in_specs=[pl.BlockSpec((None, tr, cols), lambda l, i: (l, i, 0))],
        out_specs=pl.BlockSpec((None, tr, out_cols), lambda l, i: (l, i, 0)),
        out_shape=jax.ShapeDtypeStruct((depth, rows, out_cols), BF16),
        compiler_params=pltpu.CompilerParams(
            dimension_semantics=("arbitrary", "arbitrary"), vmem_limit_bytes=VMEM_LIMIT),
        name="cast_bf16",
    )(w)


def _mod_kernel(c_ref, w_ref, b_ref, o_ref):
    ca = jax.nn.silu(c_ref[...]).astype(BF16)
    o_ref[...] = jnp.dot(ca, w_ref[...].astype(BF16), preferred_element_type=F32) + b_ref[...]


def _adaln_mod(c_pad, w_ada, b_ada):
    depth, d, n = w_ada.shape
    rows = c_pad.shape[0]
    tn = 1024
    return pl.pallas_call(
        _mod_kernel,
        grid=(depth, n // tn),
        in_specs=[
            pl.BlockSpec((rows, d), lambda l, j: (0, 0)),
            pl.BlockSpec((None, d, tn), lambda l, j: (l, 0, j)),
            pl.BlockSpec((None, 1, tn), lambda l, j: (l, 0, j)),
        ],
        out_specs=pl.BlockSpec((None, rows, tn), lambda l, j: (l, 0, j)),
        out_shape=jax.ShapeDtypeStruct((depth, rows, n), F32),
        compiler_params=pltpu.CompilerParams(
            dimension_semantics=("arbitrary", "arbitrary"), vmem_limit_bytes=VMEM_LIMIT),
        name="adaln_mod",
    )(c_pad, w_ada, b_ada.reshape(depth, 1, n))


def _rope_table_kernel(pos_ref, inv_ref, sign_ref, cos_ref, sin_ref):
    ang = pos_ref[...].astype(F32) * inv_ref[...]
    cos_ref[...] = jnp.cos(ang)
    sin_ref[...] = jnp.sin(ang) * sign_ref[...]


def _rope_tables(positions):
    t = positions.size
    tm = 2048
    inv = ROPE_BASE ** (-jnp.arange(0, ROPE_DIM, 2, dtype=F32) / ROPE_DIM)
    inv_l = jnp.tile(inv, LANES // (ROPE_DIM // 2)).reshape(1, LANES)
    sign = np.where(np.arange(LANES) % ROPE_DIM < ROPE_DIM // 2, -1.0, 1.0).astype(np.float32)
    return pl.pallas_call(
        _rope_table_kernel,
        grid=(t // tm,),
        in_specs=[
            pl.BlockSpec((tm, 1), lambda i: (i, 0)),
            pl.BlockSpec((1, LANES), lambda i: (0, 0)),
            pl.BlockSpec((1, LANES), lambda i: (0, 0)),
        ],
        out_specs=[pl.BlockSpec((tm, LANES), lambda i: (i, 0))] * 2,
        out_shape=[jax.ShapeDtypeStruct((t, LANES), F32)] * 2,
        compiler_params=pltpu.CompilerParams(dimension_semantics=("arbitrary",)),
        name="rope_tables",
    )(positions.reshape(t, 1), inv_l, jnp.asarray(sign).reshape(1, LANES))


def _ffn_kernel(x_ref, mod_ref, gain_ref, wg_ref, wu_ref, wd_ref, fgain_ref, o_ref,
                h_sc, acc_sc, *, mod_base, final_norm):
    j = pl.program_id(1)

    @pl.when(j == 0)
    def _():
        y = _rms(x_ref[...], gain_ref[...])
        shift = mod_ref[mod_base:mod_base + 1, :]
        scale = mod_ref[mod_base + 1:mod_base + 2, :]
        h_sc[...] = (y * (1 + scale) + shift).astype(BF16)
        acc_sc[...] = jnp.zeros_like(acc_sc)

    h = h_sc[...]
    g = jnp.dot(h, wg_ref[...], preferred_element_type=F32)
    u = jnp.dot(h, wu_ref[...], preferred_element_type=F32)
    a = (jax.nn.silu(g) * u).astype(BF16)
    acc_sc[...] += jnp.dot(a, wd_ref[...], preferred_element_type=F32)

    @pl.when(j == pl.num_programs(1) - 1)
    def _():
        gate = mod_ref[mod_base + 2:mod_base + 3, :]
        out = x_ref[...] + 0.5 * gate * acc_sc[...]
        if final_norm:
            out = _rms(out, fgain_ref[...])
        o_ref[...] = out


def _ffn(x, mod, gain, w_gu, w_down, final_gain, *, layer, mod_base, final_norm, seq):
    t, d = x.shape
    f = w_down.shape[1]
    tm, tf = 512, 512
    nf = f // tf
    kern = functools.partial(_ffn_kernel, mod_base=mod_base, final_norm=final_norm)
    return pl.pallas_call(
        kern,
        grid=(t // tm, nf),
        in_specs=[
            pl.BlockSpec((tm, d), lambda i, j: (i, 0)),
            pl.BlockSpec((None, None, N_MOD, d), lambda i, j: (layer, i // (seq // tm), 0, 0)),
            pl.BlockSpec((1, d), lambda i, j: (0, 0)),
            pl.BlockSpec((None, d, tf), lambda i, j: (layer, 0, j)),
            pl.BlockSpec((None, d, tf), lambda i, j: (layer, 0, nf + j)),
            pl.BlockSpec((None, tf, d), lambda i, j: (layer, j, 0)),
            pl.BlockSpec((1, d), lambda i, j: (0, 0)),
        ],
        out_specs=pl.BlockSpec((tm, d), lambda i, j: (i, 0)),
        out_shape=jax.ShapeDtypeStruct((t, d), F32),
        scratch_shapes=[pltpu.VMEM((tm, d), BF16), pltpu.VMEM((tm, d), F32)],
        compiler_params=pltpu.CompilerParams(
            dimension_semantics=("arbitrary", "arbitrary"), vmem_limit_bytes=VMEM_LIMIT),
        name="adaln_ffn",
    )(x, mod, gain.reshape(1, d), w_gu, w_gu, w_down, final_gain.reshape(1, d))


def _inproj_kernel(x_ref, mod_ref, gain_ref, cos_ref, sin_ref, win_ref, qg_ref, wuq_ref,
                   kvg_ref, wukv_ref, qk_ref, v_ref, g_ref, qm_ref, km_ref, vm_ref):
    y = _rms(x_ref[...], gain_ref[...])
    h = (y * (1 + mod_ref[4:5, :]) + mod_ref[3:4, :]).astype(BF16)
    cos = cos_ref[...]
    sin = sin_ref[...]

    def proj(c0, c1):
        return jnp.dot(h, win_ref[:, c0:c1], preferred_element_type=F32)

    half = IN_QK // 2
    for c0, mult in ((0, 1.0), (half, RET_DK ** -0.5)):
        p = proj(c0, c0 + half)
        for t in range(half // LANES):
            r = _rope(p[:, t * LANES:(t + 1) * LANES], cos, sin)
            qk_ref[:, c0 + t * LANES:c0 + (t + 1) * LANES] = (r * mult).astype(BF16)
    v_ref[...] = proj(IN_V0, IN_G0).astype(BF16)
    g_ref[...] = proj(IN_G0, IN_CQ0).astype(BF16)

    cq = _rms(proj(IN_CQ0, IN_CKV0), qg_ref[...]).astype(BF16)
    q = jnp.dot(cq, wuq_ref[...], preferred_element_type=F32)
    qscale = (MLA_NOPE + MLA_ROPE) ** -0.5 * LOG2_E
    for hd in range(MLA_HEADS):
        b0 = hd * MLA_QK_PAD
        qm_ref[:, b0:b0 + LANES] = (q[:, b0:b0 + LANES] * qscale).astype(BF16)
        r = _rope(q[:, b0 + LANES:b0 + 2 * LANES], cos, sin)
        qm_ref[:, b0 + LANES:b0 + 2 * LANES] = (r * qscale).astype(BF16)

    tail = proj(IN_CKV0, IN_COLS_PAD)
    ckv = _rms(tail[:, :MLA_KV_RANK], kvg_ref[...]).astype(BF16)
    kpe = _rope(tail[:, MLA_KV_RANK:], cos, sin).astype(BF16)
    kv = jnp.dot(ckv, wukv_ref[...], preferred_element_type=F32)
    for hd in range(MLA_HEADS):
        b0 = hd * (MLA_NOPE + MLA_DV)
        km_ref[:, hd * MLA_QK_PAD:hd * MLA_QK_PAD + LANES] = kv[:, b0:b0 + MLA_NOPE].astype(BF16)
        km_ref[:, hd * MLA_QK_PAD + LANES:(hd + 1) * MLA_QK_PAD] = kpe
        vm_ref[hd * MLA_DV:(hd + 1) * MLA_DV, :] = (
            kv[:, b0 + MLA_NOPE:b0 + MLA_NOPE + MLA_DV].T.astype(BF16))


def _inproj(x, mod, gain, cos, sin, w_in, q_gain, w_uq, kv_gain, w_ukv, *, layer, seq):
    t, d = x.shape
    tm = 256
    const = lambda i: (0, 0)
    row = lambda i: (i, 0)

    def resident(w):
        return pl.BlockSpec((None,) + w.shape[1:], lambda i: (layer, 0, 0),
                            pipeline_mode=pl.Buffered(1))

    widths = [IN_QK, RET_WIDTH, RET_WIDTH, MLA_HEADS * MLA_QK_PAD, MLA_HEADS * MLA_QK_PAD]
    tpb = seq // tm
    sub = ATTN_BLK // tm
    vt_spec = pl.BlockSpec((None, None, MLA_WIDTH, tm),
                           lambda i: (i // tpb, (i % tpb) // sub, 0, i % sub))
    vt_shape = jax.ShapeDtypeStruct((t // seq, seq // ATTN_BLK, MLA_WIDTH, ATTN_BLK), BF16)
    return pl.pallas_call(
        _inproj_kernel,
        grid=(t // tm,),
        in_specs=[
            pl.BlockSpec((tm, d), row),
            pl.BlockSpec((None, None, N_MOD, d), lambda i: (layer, i // (seq // tm), 0, 0)),
            pl.BlockSpec((1, d), const),
            pl.BlockSpec((tm, LANES), row),
            pl.BlockSpec((tm, LANES), row),
            resident(w_in),
            pl.BlockSpec((1, MLA_Q_RANK), const),
            resident(w_uq),
            pl.BlockSpec((1, MLA_KV_RANK), const),
            resident(w_ukv),
        ],
        out_specs=[pl.BlockSpec((tm, w), row) for w in widths] + [vt_spec],
        out_shape=[jax.ShapeDtypeStruct((t, w), BF16) for w in widths] + [vt_shape],
        compiler_params=pltpu.CompilerParams(
            dimension_semantics=("arbitrary",), vmem_limit_bytes=VMEM_LIMIT),
        name="mixer_inproj",
    )(x, mod, gain.reshape(1, d), cos, sin, w_in, q_gain.reshape(1, -1), w_uq,
      kv_gain.reshape(1, -1), w_ukv)


def _retention_kernel(qk_ref, v_ref, g_ref, intra_ref, qdec_ref, kdec_ref, cdec_ref, hmask_ref,
                      gain_ref, o_ref, state_sc, *, chunks):
    @pl.when(pl.program_id(1) == 0)
    def _():
        state_sc[...] = jnp.zeros_like(state_sc)

    kbase = RET_HEADS * RET_DK

    def chunk_body(c, carry):
        rows = pl.ds(pl.multiple_of(c * RET_CHUNK, RET_CHUNK), RET_CHUNK)
        for pair in range(RET_HEADS // 2):
            qf = qk_ref[rows, pair * LANES:(pair + 1) * LANES].astype(F32)
            kp = qk_ref[rows, kbase + pair * LANES:kbase + (pair + 1) * LANES]
            kdt = (kp.astype(F32) * kdec_ref[pair]).T.astype(BF16)
            for hh in range(2):
                hd = 2 * pair + hh
                cols = slice(hd * RET_DV, (hd + 1) * RET_DV)
                vh = v_ref[rows, cols]
                qm = (qf * hmask_ref[hh]).astype(BF16)
                s = lax.dot_general(qm, kp, (((1,), (1,)), ((), ())), preferred_element_type=F32)
                intra = jnp.dot((s * intra_ref[hd]).astype(BF16), vh, preferred_element_type=F32)
                qd = (qf * qdec_ref[hd]).astype(BF16)
                state = state_sc[hd]
                cross = jnp.dot(qd, state.astype(BF16), preferred_element_type=F32)
                state_sc[hd] = state * cdec_ref[hd] + jnp.dot(kdt, vh, preferred_element_type=F32)
                yh = _rms(intra + cross, gain_ref[:, cols])
                o_ref[rows, cols] = (jax.nn.silu(g_ref[rows, cols].astype(F32)) * yh).astype(BF16)
        return carry

    lax.fori_loop(0, chunks, chunk_body, 0)


def _retention_consts():
    c = RET_CHUNK
    log_g = jnp.log1p(-(2.0 ** (-5.0 - jnp.arange(RET_HEADS, dtype=F32))))
    idx = jnp.arange(c, dtype=F32)
    rel = idx[:, None] - idx[None, :]
    intra = jnp.where(rel >= 0, jnp.exp(log_g[:, None, None] * jnp.maximum(rel, 0.0)), 0.0)
    q_decay = jnp.exp(log_g[:, None] * (idx + 1.0))
    k_decay = jnp.exp(log_g[:, None] * (c - 1.0 - idx))
    chunk_decay = jnp.exp(log_g * c)
    own = (np.arange(LANES)[None, :] // RET_DK == np.arange(2)[:, None]).astype(np.float32)
    hmask = jnp.asarray(own).reshape(2, 1, LANES)
    qdec = q_decay[:, :, None] * jnp.asarray(own)[jnp.arange(RET_HEADS) % 2][:, None, :]
    kdec = jnp.repeat(k_decay.reshape(RET_HEADS // 2, 2, c), RET_DK, axis=1).transpose(0, 2, 1)
    cdec = jnp.broadcast_to(chunk_decay[:, None, None], (RET_HEADS, 1, LANES))
    return intra, qdec, kdec, cdec, hmask


def _retention(qk, v, g, gain, *, batch, seq):
    tb = 1024
    chunks = tb // RET_CHUNK
    nb = seq // tb
    intra, qdec, kdec, cdec, hmask = _retention_consts()
    row = lambda b, i: (b * nb + i, 0)
    c3 = lambda b, i: (0, 0, 0)
    return pl.pallas_call(
        functools.partial(_retention_kernel, chunks=chunks),
        grid=(batch, nb),
        in_specs=[
            pl.BlockSpec((tb, IN_QK), row),
            pl.BlockSpec((tb, RET_WIDTH), row),
            pl.BlockSpec((tb, RET_WIDTH), row),
            pl.BlockSpec(intra.shape, c3),
            pl.BlockSpec(qdec.shape, c3),
            pl.BlockSpec(kdec.shape, c3),
            pl.BlockSpec(cdec.shape, c3),
            pl.BlockSpec(hmask.shape, c3),
            pl.BlockSpec((1, RET_WIDTH), lambda b, i: (0, 0)),
        ],
        out_specs=pl.BlockSpec((tb, RET_WIDTH), row),
        out_shape=jax.ShapeDtypeStruct((batch * seq, RET_WIDTH), BF16),
        scratch_shapes=[pltpu.VMEM((RET_HEADS, LANES, RET_DV), F32)],
        compiler_params=pltpu.CompilerParams(
            dimension_semantics=("arbitrary", "arbitrary"), vmem_limit_bytes=VMEM_LIMIT),
        name="retention",
    )(qk, v, g, intra, qdec, kdec, cdec, hmask, gain.reshape(1, RET_WIDTH))


def _attn_kernel(q_ref, qnext_ref, k_ref, vt_ref, o_ref, s_sc, acc_sc, m_sc, l_sc):
    qi = pl.program_id(2)
    blk = ATTN_BLK
    acc_sc[...] = jnp.zeros_like(acc_sc)

    def scores(ki, slot, queries=q_ref):
        rows = pl.ds(pl.multiple_of(ki * blk, blk), blk)
        for hd in range(ATTN_HEADS):
            cols = slice(hd * MLA_QK_PAD, (hd + 1) * MLA_QK_PAD)
            s_sc[slot, hd] = lax.dot_general(k_ref[rows, cols], queries[:, cols],
                                             (((1,), (1,)), ((), ())), preferred_element_type=F32)

    def update(ki, slot, stats, masked):
        new_stats = []
        for hd in range(ATTN_HEADS):
            m_prev, l_prev = stats[hd]
            s = s_sc[slot, hd]
            if masked:
                kpos = lax.broadcasted_iota(jnp.int32, s.shape, 0)
                qpos = lax.broadcasted_iota(jnp.int32, s.shape, 1)
                s = jnp.where(kpos <= qpos, s, -jnp.inf)
            m_new = jnp.maximum(m_prev, jnp.max(s, axis=0, keepdims=True))
            p = jnp.exp2(s - m_new)
            alpha = jnp.exp2(m_prev - m_new)
            l_new = alpha * l_prev + jnp.sum(p, axis=0, keepdims=True)
            vt = vt_ref[ki, hd * MLA_DV:(hd + 1) * MLA_DV, :]
            acc_sc[hd] = alpha * acc_sc[hd] + jnp.dot(vt, p.astype(BF16), preferred_element_type=F32)
            new_stats.append((m_new, l_new))
        return tuple(new_stats)

    def load_stats():
        return tuple((m_sc[hd], l_sc[hd]) for hd in range(ATTN_HEADS))

    def store_stats(stats):
        for hd in range(ATTN_HEADS):
            m_sc[hd], l_sc[hd] = stats[hd]

    @pl.when(qi == 0)
    def _():
        scores(0, 0)

    def block_pair(kp, stats):
        scores(2 * kp + 1, 1)
        stats = update(2 * kp, 0, stats, False)
        scores(2 * kp + 2, 0)
        stats = update(2 * kp + 1, 1, stats, False)
        return stats

    init = tuple((jnp.full((1, blk), -jnp.inf, F32), jnp.zeros((1, blk), F32))
                 for _ in range(ATTN_HEADS))
    store_stats(lax.fori_loop(0, qi // 2, block_pair, init))

    @pl.when(qi % 2 == 1)
    def _():
        stats = update(qi - 1, 0, load_stats(), False)
        scores(qi, 1)
        store_stats(stats)

    stats = update(qi, qi % 2, load_stats(), True)
    scores(0, 0, qnext_ref)
    for hd in range(ATTN_HEADS):
        out_t = acc_sc[hd] / stats[hd][1]
        o_ref[:, hd * MLA_DV:(hd + 1) * MLA_DV] = out_t.T.astype(BF16)


def _attention(qm, km, vt, *, batch, seq):
    blk = ATTN_BLK
    nq = seq // blk
    qm = qm.reshape(batch, seq, -1)
    km = km.reshape(batch, seq, -1)
    out = pl.pallas_call(
        _attn_kernel,
        grid=(batch, MLA_HEADS // ATTN_HEADS, nq),
        in_specs=[
            pl.BlockSpec((None, blk, ATTN_HEADS * MLA_QK_PAD), lambda b, h, i: (b, i, h)),
            pl.BlockSpec((None, blk, ATTN_HEADS * MLA_QK_PAD),
                         lambda b, h, i: (b, jnp.minimum(i + 1, nq - 1), h)),
            pl.BlockSpec((None, seq, ATTN_HEADS * MLA_QK_PAD), lambda b, h, i: (b, 0, h)),
            pl.BlockSpec((None, nq, ATTN_HEADS * MLA_DV, blk), lambda b, h, i: (b, 0, h, 0)),
        ],
        out_specs=pl.BlockSpec((None, blk, ATTN_HEADS * MLA_DV), lambda b, h, i: (b, i, h)),
        out_shape=jax.ShapeDtypeStruct((batch, seq, MLA_WIDTH), BF16),
        scratch_shapes=[pltpu.VMEM((2, ATTN_HEADS, blk, blk), F32),
                        pltpu.VMEM((ATTN_HEADS, MLA_DV, blk), F32),
                        pltpu.VMEM((ATTN_HEADS, 1, blk), F32), pltpu.VMEM((ATTN_HEADS, 1, blk), F32)],
        compiler_params=pltpu.CompilerParams(
            dimension_semantics=("arbitrary", "arbitrary", "arbitrary"),
            vmem_limit_bytes=VMEM_LIMIT),
        name="mla_attention",
    )(qm, qm, km, vt)
    return out.reshape(batch * seq, MLA_WIDTH)


def _outproj_kernel(x_ref, mod_ref, yr_ref, ym_ref, w_ref, o_ref):
    y = jnp.dot(yr_ref[...], w_ref[:RET_WIDTH, :], preferred_element_type=F32)
    y += jnp.dot(ym_ref[...], w_ref[RET_WIDTH:, :], preferred_element_type=F32)
    o_ref[...] = x_ref[...] + mod_ref[5:6, :] * y


def _outproj(x, mod, y_r, y_m, w_out, *, layer, seq):
    t, d = x.shape
    tm = 512
    row = lambda i: (i, 0)
    return pl.pallas_call(
        _outproj_kernel,
        grid=(t // tm,),
        in_specs=[
            pl.BlockSpec((tm, d), row),
            pl.BlockSpec((None, None, N_MOD, d), lambda i: (layer, i // (seq // tm), 0, 0)),
            pl.BlockSpec((tm, RET_WIDTH), row),
            pl.BlockSpec((tm, MLA_WIDTH), row),
            pl.BlockSpec((None,) + w_out.shape[1:], lambda i: (layer, 0, 0),
                         pipeline_mode=pl.Buffered(1)),
        ],
        out_specs=pl.BlockSpec((tm, d), row),
        out_shape=jax.ShapeDtypeStruct((t, d), F32),
        compiler_params=pltpu.CompilerParams(
            dimension_semantics=("arbitrary",), vmem_limit_bytes=VMEM_LIMIT),
        name="mixer_outproj",
    )(x, mod, y_r, y_m, w_out)


def kernel(x, c, positions, w_ada, b_ada, norm_ffn1, ffn1_w_gu, ffn1_w_down, norm_mix, w_in,
           ret_norm_g, q_norm_g, w_uq, kv_norm_g, w_ukv, w_out, norm_ffn2, ffn2_w_gu,
           ffn2_w_down, final_norm):
    batch, seq, d = x.shape
    depth = w_ada.shape[0]
    xt = x.reshape(batch * seq, d)

    c_pad = jnp.pad(c, ((0, 8 - batch), (0, 0)))
    mod = _adaln_mod(c_pad, w_ada, b_ada)[:, :batch].reshape(depth, batch, N_MOD, d)
    cos, sin = _rope_tables(positions)

    w_in_b = _to_bf16(w_in, IN_COLS_PAD)
    w_uq_b = jnp.pad(w_uq.reshape(depth, MLA_Q_RANK, MLA_HEADS, MLA_NOPE + MLA_ROPE),
                     ((0, 0), (0, 0), (0, 0), (0, MLA_QK_PAD - MLA_NOPE - MLA_ROPE))
                     ).reshape(depth, MLA_Q_RANK, MLA_HEADS * MLA_QK_PAD).astype(BF16)
    w_ukv_b = _to_bf16(w_ukv)
    w_out_b = _to_bf16(w_out)
    ffn_w = [(_to_bf16(ffn1_w_gu), _to_bf16(ffn1_w_down)), (_to_bf16(ffn2_w_gu), _to_bf16(ffn2_w_down))]
    ffn_gain = [norm_ffn1, norm_ffn2]

    def ffn(xt, l, which):
        w_gu, w_down = ffn_w[which]
        return _ffn(xt, mod, ffn_gain[which][l], w_gu, w_down, final_norm, layer=l,
                    mod_base=6 * which, final_norm=(which == 1 and l == depth - 1), seq=seq)

    for l in range(depth):
        xt = ffn(xt, l, 0)
        qk, v_r, g_r, qm, km, vt = _inproj(xt, mod, norm_mix[l], cos, sin, w_in_b, q_norm_g[l],
                                           w_uq_b, kv_norm_g[l], w_ukv_b, layer=l, seq=seq)
        y_r = _retention(qk, v_r, g_r, ret_norm_g[l], batch=batch, seq=seq)
        y_m = _attention(qm, km, vt, batch=batch, seq=seq)
        xt = _outproj(xt, mod, y_r, y_m, w_out_b, layer=l, seq=seq)
        xt = ffn(xt, l, 1)
    return xt.reshape(batch, seq, d)
```

```python
import functools

import jax
import jax.numpy as jnp
import numpy as np
from jax import lax
from jax.experimental import pallas as pl
from jax.experimental.pallas import tpu as pltpu

D_MODEL = 2048
DEPTH = 2
RET_HEADS = 8
RET_DK = 64
RET_DV = 128
RET_CHUNK = 128
MLA_HEADS = 8
MLA_Q_RANK = 512
MLA_KV_RANK = 256
MLA_NOPE = 128
MLA_ROPE = 64
MLA_DV = 128
RET_WIDTH = RET_HEADS * RET_DV
MLA_WIDTH = MLA_HEADS * MLA_DV
ROPE_DIM = 64
ROPE_BASE = 10000.0
EPS = 1e-6
N_MOD = 9

LANES = 128
MLA_QK_PAD = 256
IN_QK = 2 * RET_HEADS * RET_DK
IN_V0 = IN_QK
IN_G0 = IN_V0 + RET_WIDTH
IN_CQ0 = IN_G0 + RET_WIDTH
IN_CKV0 = IN_CQ0 + MLA_Q_RANK
IN_KPE0 = IN_CKV0 + MLA_KV_RANK
IN_COLS_PAD = IN_KPE0 + LANES

VMEM_LIMIT = 56 * 1024 * 1024
ATTN_BLK = 512
ATTN_HEADS = 4
LOG2_E = 1.4426950408889634
CAST_BLOCK_BYTES = 8 * 1024 * 1024

BF16 = jnp.bfloat16
F32 = jnp.float32


def _rms(xf, gain):
    return xf * lax.rsqrt(jnp.mean(xf * xf, axis=-1, keepdims=True) + EPS) * gain


def _rope(blk, cos, sin_signed):
    lane = lax.broadcasted_iota(jnp.int32, blk.shape, 1)
    partner = jnp.where(lane % ROPE_DIM < ROPE_DIM // 2,
                        pltpu.roll(blk, LANES - ROPE_DIM // 2, 1),
                        pltpu.roll(blk, ROPE_DIM // 2, 1))
    return blk * cos + partner * sin_signed


def _cast_kernel(w_ref, o_ref):
    cols = w_ref.shape[-1]
    o_ref[:, :cols] = w_ref[...].astype(BF16)
    if o_ref.shape[-1] > cols:
        o_ref[:, cols:] = jnp.zeros((o_ref.shape[0], o_ref.shape[-1] - cols), BF16)


def _to_bf16(w, out_cols=None):
    depth, rows, cols = w.shape
    out_cols = out_cols or cols
    tr = 16
    while rows % (2 * tr) == 0 and 2 * tr * cols * 4 <= CAST_BLOCK_BYTES:
        tr *= 2
    return pl.pallas_call(
        _cast_kernel,
        grid=(depth, rows // tr),
        in_specs=[pl.BlockSpec((None, tr, cols), lambda l, i: (l, i, 0))],
        out_specs=pl.BlockSpec((None, tr, out_cols), lambda l, i: (l, i, 0)),
        out_shape=jax.ShapeDtypeStruct((depth, rows, out_cols), BF16),
        compiler_params=pltpu.CompilerParams(
            dimension_semantics=("arbitrary", "arbitrary"), vmem_limit_bytes=VMEM_LIMIT),
        name="cast_bf16",
    )(w)


def _mod_kernel(c_ref, w_ref, b_ref, o_ref):
    ca = jax.nn.silu(c_ref[...]).astype(BF16)
    o_ref[...] = jnp.dot(ca, w_ref[...].astype(BF16), preferred_element_type=F32) + b_ref[...]


def _adaln_mod(c_pad, w_ada, b_ada):
    depth, d, n = w_ada.shape
    rows = c_pad.shape[0]
    tn = 1024
    return pl.pallas_call(
        _mod_kernel,
        grid=(depth, n // tn),
        in_specs=[
            pl.BlockSpec((rows, d), lambda l, j: (0, 0)),
            pl.BlockSpec((None, d, tn), lambda l, j: (l, 0, j)),
            pl.BlockSpec((None, 1, tn), lambda l, j: (l, 0, j)),
        ],
        out_specs=pl.BlockSpec((None, rows, tn), lambda l, j: (l, 0, j)),
        out_shape=jax.ShapeDtypeStruct((depth, rows, n), F32),
        compiler_params=pltpu.CompilerParams(
            dimension_semantics=("arbitrary", "arbitrary"), vmem_limit_bytes=VMEM_LIMIT),
        name="adaln_mod",
    )(c_pad, w_ada, b_ada.reshape(depth, 1, n))


def _rope_table_kernel(pos_ref, inv_ref, sign_ref, cos_ref, sin_ref):
    ang = pos_ref[...].astype(F32) * inv_ref[...]
    cos_ref[...] = jnp.cos(ang)
    sin_ref[...] = jnp.sin(ang) * sign_ref[...]


def _rope_tables(positions):
    t = positions.size
    tm = 2048
    inv = ROPE_BASE ** (-jnp.arange(0, ROPE_DIM, 2, dtype=F32) / ROPE_DIM)
    inv_l = jnp.tile(inv, LANES // (ROPE_DIM // 2)).reshape(1, LANES)
    sign = np.where(np.arange(LANES) % ROPE_DIM < ROPE_DIM // 2, -1.0, 1.0).astype(np.float32)
    return pl.pallas_call(
        _rope_table_kernel,
        grid=(t // tm,),
        in_specs=[
            pl.BlockSpec((tm, 1), lambda i: (i, 0)),
            pl.BlockSpec((1, LANES), lambda i: (0, 0)),
            pl.BlockSpec((1, LANES), lambda i: (0, 0)),
        ],
        out_specs=[pl.BlockSpec((tm, LANES), lambda i: (i, 0))] * 2,
        out_shape=[jax.ShapeDtypeStruct((t, LANES), F32)] * 2,
        compiler_params=pltpu.CompilerParams(dimension_semantics=("arbitrary",)),
        name="rope_tables",
    )(positions.reshape(t, 1), inv_l, jnp.asarray(sign).reshape(1, LANES))


def _ffn_kernel(x_ref, xnext_ref, mod_ref, modnext_ref, gain_ref, wg_ref, wu_ref, wd_ref,
                fgain_ref, o_ref, h_sc, acc_sc, *, mod_base, final_norm, chunk):
    i = pl.program_id(0)
    j = pl.program_id(1)
    tm = x_ref.shape[0]
    slot = i % 2

    def norm_mod(xf, m_ref):
        y = _rms(xf, gain_ref[...])
        return (y * (1 + m_ref[mod_base + 1:mod_base + 2, :]) + m_ref[mod_base:mod_base + 1, :]
                ).astype(BF16)

    @pl.when((i == 0) & (j == 0))
    def _():
        h_sc[0] = norm_mod(x_ref[...], mod_ref)
        acc_sc[...] = jnp.zeros_like(acc_sc)

    h = h_sc[slot]
    g = jnp.dot(h, wg_ref[...], preferred_element_type=F32)
    u = jnp.dot(h, wu_ref[...], preferred_element_type=F32)
    a = (jax.nn.silu(g) * u).astype(BF16)
    acc_sc[...] = jnp.where(j > 0, acc_sc[...], 0.0) + jnp.dot(a, wd_ref[...],
                                                               preferred_element_type=F32)

    rows = pl.ds(pl.multiple_of(jnp.minimum(j * chunk, tm - chunk), 16), chunk)
    h_sc[1 - slot, rows, :] = norm_mod(xnext_ref[rows, :], modnext_ref)

    @pl.when(j == pl.num_programs(1) - 1)
    def _():
        gate = mod_ref[mod_base + 2:mod_base + 3, :]
        out = x_ref[...] + 0.5 * gate * acc_sc[...]
        if final_norm:
            out = _rms(out, fgain_ref[...])
        o_ref[...] = out


def _ffn(x, mod, gain, w_gu, w_down, final_gain, *, layer, mod_base, final_norm, seq):
    t, d = x.shape
    f = w_down.shape[1]
    tm, tf = 512, 512
    nf = f // tf
    nt = t // tm
    chunk = -(-tm // (16 * nf)) * 16
    assert chunk <= tm
    kern = functools.partial(_ffn_kernel, mod_base=mod_base, final_norm=final_norm, chunk=chunk)
    nxt = lambda i: jnp.minimum(i + 1, nt - 1)
    return pl.pallas_call(
        kern,
        grid=(nt, nf),
        in_specs=[
            pl.BlockSpec((tm, d), lambda i, j: (i, 0)),
            pl.BlockSpec((tm, d), lambda i, j: (nxt(i), 0)),
            pl.BlockSpec((None, None, N_MOD, d), lambda i, j: (layer, i // (seq // tm), 0, 0)),
            pl.BlockSpec((None, None, N_MOD, d), lambda i, j: (layer, nxt(i) // (seq // tm), 0, 0)),
            pl.BlockSpec((1, d), lambda i, j: (0, 0)),
            pl.BlockSpec((None, d, tf), lambda i, j: (layer, 0, j)),
            pl.BlockSpec((None, d, tf), lambda i, j: (layer, 0, nf + j)),
            pl.BlockSpec((None, tf, d), lambda i, j: (layer, j, 0)),
            pl.BlockSpec((1, d), lambda i, j: (0, 0)),
        ],
        out_specs=pl.BlockSpec((tm, d), lambda i, j: (i, 0)),
        out_shape=jax.ShapeDtypeStruct((t, d), F32),
        scratch_shapes=[pltpu.VMEM((2, tm, d), BF16), pltpu.VMEM((tm, d), F32)],
        compiler_params=pltpu.CompilerParams(
            dimension_semantics=("arbitrary", "arbitrary"), vmem_limit_bytes=VMEM_LIMIT),
        name="adaln_ffn",
    )(x, x, mod, mod, gain.reshape(1, d), w_gu, w_gu, w_down, final_gain.reshape(1, d))


def _inproj_kernel(x_ref, mod_ref, gain_ref, cos_ref, sin_ref, win_ref, qg_ref, wuq_ref,
                   kvg_ref, wukv_ref, qk_ref, v_ref, g_ref, qm_ref, km_ref, vm_ref):
    y = _rms(x_ref[...], gain_ref[...])
    h = (y * (1 + mod_ref[4:5, :]) + mod_ref[3:4, :]).astype(BF16)
    cos = cos_ref[...]
    sin = sin_ref[...]

    def proj(c0, c1):
        return jnp.dot(h, win_ref[:, c0:c1], preferred_element_type=F32)

    half = IN_QK // 2
    for c0, mult in ((0, 1.0), (half, RET_DK ** -0.5)):
        p = proj(c0, c0 + half)
        for t in range(half // LANES):
            r = _rope(p[:, t * LANES:(t + 1) * LANES], cos, sin)
            qk_ref[:, c0 + t * LANES:c0 + (t + 1) * LANES] = (r * mult).astype(BF16)
    v_ref[...] = proj(IN_V0, IN_G0).astype(BF16)
    g_ref[...] = proj(IN_G0, IN_CQ0).astype(BF16)

    cq = _rms(proj(IN_CQ0, IN_CKV0), qg_ref[...]).astype(BF16)
    q = jnp.dot(cq, wuq_ref[...], preferred_element_type=F32)
    qscale = (MLA_NOPE + MLA_ROPE) ** -0.5 * LOG2_E
    for hd in range(MLA_HEADS):
        b0 = hd * MLA_QK_PAD
        qm_ref[:, b0:b0 + LANES] = (q[:, b0:b0 + LANES] * qscale).astype(BF16)
        r = _rope(q[:, b0 + LANES:b0 + 2 * LANES], cos, sin)
        qm_ref[:, b0 + LANES:b0 + 2 * LANES] = (r * qscale).astype(BF16)

    tail = proj(IN_CKV0, IN_COLS_PAD)
    ckv = _rms(tail[:, :MLA_KV_RANK], kvg_ref[...]).astype(BF16)
    kpe = _rope(tail[:, MLA_KV_RANK:], cos, sin).astype(BF16)
    kv = jnp.dot(ckv, wukv_ref[...], preferred_element_type=F32)
    for hd in range(MLA_HEADS):
        b0 = hd * (MLA_NOPE + MLA_DV)
        km_ref[:, hd * MLA_QK_PAD:hd * MLA_QK_PAD + LANES] = kv[:, b0:b0 + MLA_NOPE].astype(BF16)
        km_ref[:, hd * MLA_QK_PAD + LANES:(hd + 1) * MLA_QK_PAD] = kpe
        vm_ref[hd * MLA_DV:(hd + 1) * MLA_DV, :] = (
            kv[:, b0 + MLA_NOPE:b0 + MLA_NOPE + MLA_DV].T.astype(BF16))


def _inproj(x, mod, gain, cos, sin, w_in, q_gain, w_uq, kv_gain, w_ukv, *, layer, seq):
    t, d = x.shape
    tm = 256
    const = lambda i: (0, 0)
    row = lambda i: (i, 0)

    def resident(w):
        return pl.BlockSpec((None,) + w.shape[1:], lambda i: (layer, 0, 0),
                            pipeline_mode=pl.Buffered(1))

    widths = [IN_QK, RET_WIDTH, RET_WIDTH, MLA_HEADS * MLA_QK_PAD, MLA_HEADS * MLA_QK_PAD]
    tpb = seq // tm
    sub = ATTN_BLK // tm
    vt_spec = pl.BlockSpec((None, None, MLA_WIDTH, tm),
                           lambda i: (i // tpb, (i % tpb) // sub, 0, i % sub))
    vt_shape = jax.ShapeDtypeStruct((t // seq, seq // ATTN_BLK, MLA_WIDTH, ATTN_BLK), BF16)
    return pl.pallas_call(
        _inproj_kernel,
        grid=(t // tm,),
        in_specs=[
            pl.BlockSpec((tm, d), row),
            pl.BlockSpec((None, None, N_MOD, d), lambda i: (layer, i // (seq // tm), 0, 0)),
            pl.BlockSpec((1, d), const),
            pl.BlockSpec((tm, LANES), row),
            pl.BlockSpec((tm, LANES), row),
            resident(w_in),
            pl.BlockSpec((1, MLA_Q_RANK), const),
            resident(w_uq),
            pl.BlockSpec((1, MLA_KV_RANK), const),
            resident(w_ukv),
        ],
        out_specs=[pl.BlockSpec((tm, w), row) for w in widths] + [vt_spec],
        out_shape=[jax.ShapeDtypeStruct((t, w), BF16) for w in widths] + [vt_shape],
        compiler_params=pltpu.CompilerParams(
            dimension_semantics=("arbitrary",), vmem_limit_bytes=VMEM_LIMIT),
        name="mixer_inproj",
    )(x, mod, gain.reshape(1, d), cos, sin, w_in, q_gain.reshape(1, -1), w_uq,
      kv_gain.reshape(1, -1), w_ukv)


def _retention_kernel(qk_ref, v_ref, g_ref, intra_ref, qdec_ref, kdec_ref, cdec_ref, hmask_ref,
                      gain_ref, o_ref, state_sc, *, chunks):
    @pl.when(pl.program_id(1) == 0)
    def _():
        state_sc[...] = jnp.zeros_like(state_sc)

    kbase = RET_HEADS * RET_DK

    def chunk_body(c, carry):
        rows = pl.ds(pl.multiple_of(c * RET_CHUNK, RET_CHUNK), RET_CHUNK)
        for pair in range(RET_HEADS // 2):
            qf = qk_ref[rows, pair * LANES:(pair + 1) * LANES].astype(F32)
            kp = qk_ref[rows, kbase + pair * LANES:kbase + (pair + 1) * LANES]
            kdt = (kp.astype(F32) * kdec_ref[pair]).T.astype(BF16)
            for hh in range(2):
                hd = 2 * pair + hh
                cols = slice(hd * RET_DV, (hd + 1) * RET_DV)
                vh = v_ref[rows, cols]
                qm = (qf * hmask_ref[hh]).astype(BF16)
                s = lax.dot_general(qm, kp, (((1,), (1,)), ((), ())), preferred_element_type=F32)
                intra = jnp.dot((s * intra_ref[hd]).astype(BF16), vh, preferred_element_type=F32)
                qd = (qf * qdec_ref[hd]).astype(BF16)
                state = state_sc[hd]
                cross = jnp.dot(qd, state.astype(BF16), preferred_element_type=F32)
                state_sc[hd] = state * cdec_ref[hd] + jnp.dot(kdt, vh, preferred_element_type=F32)
                yh = _rms(intra + cross, gain_ref[:, cols])
                o_ref[rows, cols] = (jax.nn.silu(g_ref[rows, cols].astype(F32)) * yh).astype(BF16)
        return carry

    lax.fori_loop(0, chunks, chunk_body, 0)


def _retention_consts():
    c = RET_CHUNK
    log_g = jnp.log1p(-(2.0 ** (-5.0 - jnp.arange(RET_HEADS, dtype=F32))))
    idx = jnp.arange(c, dtype=F32)
    rel = idx[:, None] - idx[None, :]
    intra = jnp.where(rel >= 0, jnp.exp(log_g[:, None, None] * jnp.maximum(rel, 0.0)), 0.0)
    q_decay = jnp.exp(log_g[:, None] * (idx + 1.0))
    k_decay = jnp.exp(log_g[:, None] * (c - 1.0 - idx))
    chunk_decay = jnp.exp(log_g * c)
    own = (np.arange(LANES)[None, :] // RET_DK == np.arange(2)[:, None]).astype(np.float32)
    hmask = jnp.asarray(own).reshape(2, 1, LANES)
    qdec = q_decay[:, :, None] * jnp.asarray(own)[jnp.arange(RET_HEADS) % 2][:, None, :]
    kdec = jnp.repeat(k_decay.reshape(RET_HEADS // 2, 2, c), RET_DK, axis=1).transpose(0, 2, 1)
    cdec = jnp.broadcast_to(chunk_decay[:, None, None], (RET_HEADS, 1, LANES))
    return intra, qdec, kdec, cdec, hmask


def _retention(qk, v, g, gain, *, batch, seq):
    tb = 1024
    chunks = tb // RET_CHUNK
    nb = seq // tb
    intra, qdec, kdec, cdec, hmask = _retention_consts()
    row = lambda b, i: (b * nb + i, 0)
    c3 = lambda b, i: (0, 0, 0)
    return pl.pallas_call(
        functools.partial(_retention_kernel, chunks=chunks),
        grid=(batch, nb),
        in_specs=[
            pl.BlockSpec((tb, IN_QK), row),
            pl.BlockSpec((tb, RET_WIDTH), row),
            pl.BlockSpec((tb, RET_WIDTH), row),
            pl.BlockSpec(intra.shape, c3),
            pl.BlockSpec(qdec.shape, c3),
            pl.BlockSpec(kdec.shape, c3),
            pl.BlockSpec(cdec.shape, c3),
            pl.BlockSpec(hmask.shape, c3),
            pl.BlockSpec((1, RET_WIDTH), lambda b, i: (0, 0)),
        ],
        out_specs=pl.BlockSpec((tb, RET_WIDTH), row),
        out_shape=jax.ShapeDtypeStruct((batch * seq, RET_WIDTH), BF16),
        scratch_shapes=[pltpu.VMEM((RET_HEADS, LANES, RET_DV), F32)],
        compiler_params=pltpu.CompilerParams(
            dimension_semantics=("arbitrary", "arbitrary"), vmem_limit_bytes=VMEM_LIMIT),
        name="retention",
    )(qk, v, g, intra, qdec, kdec, cdec, hmask, gain.reshape(1, RET_WIDTH))


def _attn_kernel(q_ref, qnext_ref, k_ref, vt_ref, o_ref, s_sc, acc_sc, m_sc, l_sc):
    qi = pl.program_id(2)
    blk = ATTN_BLK
    acc_sc[...] = jnp.zeros_like(acc_sc)

    def scores(ki, slot, queries=q_ref):
        rows = pl.ds(pl.multiple_of(ki * blk, blk), blk)
        for hd in range(ATTN_HEADS):
            cols = slice(hd * MLA_QK_PAD, (hd + 1) * MLA_QK_PAD)
            s_sc[slot, hd] = lax.dot_general(k_ref[rows, cols], queries[:, cols],
                                             (((1,), (1,)), ((), ())), preferred_element_type=F32)

    def update(ki, slot, stats, masked):
        new_stats = []
        for hd in range(ATTN_HEADS):
            m_prev, l_prev = stats[hd]
            s = s_sc[slot, hd]
            if masked:
                kpos = lax.broadcasted_iota(jnp.int32, s.shape, 0)
                qpos = lax.broadcasted_iota(jnp.int32, s.shape, 1)
                s = jnp.where(kpos <= qpos, s, -jnp.inf)
            m_new = jnp.maximum(m_prev, jnp.max(s, axis=0, keepdims=True))
            p = jnp.exp2(s - m_new)
            alpha = jnp.exp2(m_prev - m_new)
            l_new = alpha * l_prev + jnp.sum(p, axis=0, keepdims=True)
            vt = vt_ref[ki, hd * MLA_DV:(hd + 1) * MLA_DV, :]
            acc_sc[hd] = alpha * acc_sc[hd] + jnp.dot(vt, p.astype(BF16), preferred_element_type=F32)
            new_stats.append((m_new, l_new))
        return tuple(new_stats)

    def load_stats():
        return tuple((m_sc[hd], l_sc[hd]) for hd in range(ATTN_HEADS))

    def store_stats(stats):
        for hd in range(ATTN_HEADS):
            m_sc[hd], l_sc[hd] = stats[hd]

    @pl.when(qi == 0)
    def _():
        scores(0, 0)

    def block_pair(kp, stats):
        scores(2 * kp + 1, 1)
        stats = update(2 * kp, 0, stats, False)
        scores(2 * kp + 2, 0)
        stats = update(2 * kp + 1, 1, stats, False)
        return stats

    init = tuple((jnp.full((1, blk), -jnp.inf, F32), jnp.zeros((1, blk), F32))
                 for _ in range(ATTN_HEADS))
    store_stats(lax.fori_loop(0, qi // 2, block_pair, init))

    @pl.when(qi % 2 == 1)
    def _():
        stats = update(qi - 1, 0, load_stats(), False)
        scores(qi, 1)
        store_stats(stats)

    stats = update(qi, qi % 2, load_stats(), True)
    scores(0, 0, qnext_ref)
    for hd in range(ATTN_HEADS):
        out_t = acc_sc[hd] / stats[hd][1]
        o_ref[:, hd * MLA_DV:(hd + 1) * MLA_DV] = out_t.T.astype(BF16)


def _attention(qm, km, vt, *, batch, seq):
    blk = ATTN_BLK
    nq = seq // blk
    qm = qm.reshape(batch, seq, -1)
    km = km.reshape(batch, seq, -1)
    out = pl.pallas_call(
        _attn_kernel,
        grid=(batch, MLA_HEADS // ATTN_HEADS, nq),
        in_specs=[
            pl.BlockSpec((None, blk, ATTN_HEADS * MLA_QK_PAD), lambda b, h, i: (b, i, h)),
            pl.BlockSpec((None, blk, ATTN_HEADS * MLA_QK_PAD),
                         lambda b, h, i: (b, jnp.minimum(i + 1, nq - 1), h)),
            pl.BlockSpec((None, seq, ATTN_HEADS * MLA_QK_PAD), lambda b, h, i: (b, 0, h)),
            pl.BlockSpec((None, nq, ATTN_HEADS * MLA_DV, blk), lambda b, h, i: (b, 0, h, 0)),
        ],
        out_specs=pl.BlockSpec((None, blk, ATTN_HEADS * MLA_DV), lambda b, h, i: (b, i, h)),
        out_shape=jax.ShapeDtypeStruct((batch, seq, MLA_WIDTH), BF16),
        scratch_shapes=[pltpu.VMEM((2, ATTN_HEADS, blk, blk), F32),
                        pltpu.VMEM((ATTN_HEADS, MLA_DV, blk), F32),
                        pltpu.VMEM((ATTN_HEADS, 1, blk), F32), pltpu.VMEM((ATTN_HEADS, 1, blk), F32)],
        compiler_params=pltpu.CompilerParams(
            dimension_semantics=("arbitrary", "arbitrary", "arbitrary"),
            vmem_limit_bytes=VMEM_LIMIT),
        name="mla_attention",
    )(qm, qm, km, vt)
    return out.reshape(batch * seq, MLA_WIDTH)


def _outproj_kernel(x_ref, mod_ref, yr_ref, ym_ref, w_ref, o_ref):
    y = jnp.dot(yr_ref[...], w_ref[:RET_WIDTH, :], preferred_element_type=F32)
    y += jnp.dot(ym_ref[...], w_ref[RET_WIDTH:, :], preferred_element_type=F32)
    o_ref[...] = x_ref[...] + mod_ref[5:6, :] * y


def _outproj(x, mod, y_r, y_m, w_out, *, layer, seq):
    t, d = x.shape
    tm = 512
    row = lambda i: (i, 0)
    return pl.pallas_call(
        _outproj_kernel,
        grid=(t // tm,),
        in_specs=[
            pl.BlockSpec((tm, d), row),
            pl.BlockSpec((None, None, N_MOD, d), lambda i: (layer, i // (seq // tm), 0, 0)),
            pl.BlockSpec((tm, RET_WIDTH), row),
            pl.BlockSpec((tm, MLA_WIDTH), row),
            pl.BlockSpec((None,) + w_out.shape[1:], lambda i: (layer, 0, 0),
                         pipeline_mode=pl.Buffered(1)),
        ],
        out_specs=pl.BlockSpec((tm, d), row),
        out_shape=jax.ShapeDtypeStruct((t, d), F32),
        compiler_params=pltpu.CompilerParams(
            dimension_semantics=("arbitrary",), vmem_limit_bytes=VMEM_LIMIT),
        name="mixer_outproj",
    )(x, mod, y_r, y_m, w_out)


def kernel(x, c, positions, w_ada, b_ada, norm_ffn1, ffn1_w_gu, ffn1_w_down, norm_mix, w_in,
           ret_norm_g, q_norm_g, w_uq, kv_norm_g, w_ukv, w_out, norm_ffn2, ffn2_w_gu,
           ffn2_w_down, final_norm):
    batch, seq, d = x.shape
    depth = w_ada.shape[0]
    xt = x.reshape(batch * seq, d)

    c_pad = jnp.pad(c, ((0, 8 - batch), (0, 0)))
    mod = _adaln_mod(c_pad, w_ada, b_ada)[:, :batch].reshape(depth, batch, N_MOD, d)
    cos, sin = _rope_tables(positions)

    w_in_b = jnp.pad(w_in, ((0, 0), (0, 0), (0, IN_COLS_PAD - w_in.shape[-1]))).astype(BF16)
    w_uq_b = jnp.pad(w_uq.reshape(depth, MLA_Q_RANK, MLA_HEADS, MLA_NOPE + MLA_ROPE),
                     ((0, 0), (0, 0), (0, 0), (0, MLA_QK_PAD - MLA_NOPE - MLA_ROPE))
                     ).reshape(depth, MLA_Q_RANK, MLA_HEADS * MLA_QK_PAD).astype(BF16)
    w_ukv_b = _to_bf16(w_ukv)
    w_out_b = _to_bf16(w_out)
    ffn_w = [(_to_bf16(ffn1_w_gu), _to_bf16(ffn1_w_down)), (_to_bf16(ffn2_w_gu), _to_bf16(ffn2_w_down))]
    ffn_gain = [norm_ffn1, norm_ffn2]

    def ffn(xt, l, which):
        w_gu, w_down = ffn_w[which]
        return _ffn(xt, mod, ffn_gain[which][l], w_gu, w_down, final_norm, layer=l,
                    mod_base=6 * which, final_norm=(which == 1 and l == depth - 1), seq=seq)

    for l in range(depth):
        xt = ffn(xt, l, 0)
        qk, v_r, g_r, qm, km, vt = _inproj(xt, mod, norm_mix[l], cos, sin, w_in_b, q_norm_g[l],
                                           w_uq_b, kv_norm_g[l], w_ukv_b, layer=l, seq=seq)
        y_r = _retention(qk, v_r, g_r, ret_norm_g[l], batch=batch, seq=seq)
        y_m = _attention(qm, km, vt, batch=batch, seq=seq)
        xt = _outproj(xt, mod, y_r, y_m, w_out_b, layer=l, seq=seq)
        xt = ffn(xt, l, 1)
    return xt.reshape(batch, seq, d)
```

```python
import functools

import jax
import jax.numpy as jnp
import numpy as np
from jax import lax
from jax.experimental import pallas as pl
from jax.experimental.pallas import tpu as pltpu

D_MODEL = 2048
DEPTH = 2
RET_HEADS = 8
RET_DK = 64
RET_DV = 128
RET_CHUNK = 128
MLA_HEADS = 8
MLA_Q_RANK = 512
MLA_KV_RANK = 256
MLA_NOPE = 128
MLA_ROPE = 64
MLA_DV = 128
RET_WIDTH = RET_HEADS * RET_DV
MLA_WIDTH = MLA_HEADS * MLA_DV
ROPE_DIM = 64
ROPE_BASE = 10000.0
EPS = 1e-6
N_MOD = 9

LANES = 128
MLA_QK_PAD = 256
IN_QK = 2 * RET_HEADS * RET_DK
IN_V0 = IN_QK
IN_G0 = IN_V0 + RET_WIDTH
IN_CQ0 = IN_G0 + RET_WIDTH
IN_CKV0 = IN_CQ0 + MLA_Q_RANK
IN_KPE0 = IN_CKV0 + MLA_KV_RANK
IN_COLS_PAD = IN_KPE0 + LANES

VMEM_LIMIT = 56 * 1024 * 1024
ATTN_BLK = 512
ATTN_HEADS = 4
LOG2_E = 1.4426950408889634
CAST_BLOCK_BYTES = 8 * 1024 * 1024
FFN_TM = 512
FFN_TF = 512

BF16 = jnp.bfloat16
F32 = jnp.float32


def _rms(xf, gain):
    return xf * lax.rsqrt(jnp.mean(xf * xf, axis=-1, keepdims=True) + EPS) * gain


def _rope(blk, cos, sin_signed):
    lane = lax.broadcasted_iota(jnp.int32, blk.shape, 1)
    partner = jnp.where(lane % ROPE_DIM < ROPE_DIM // 2,
                        pltpu.roll(blk, LANES - ROPE_DIM // 2, 1),
                        pltpu.roll(blk, ROPE_DIM // 2, 1))
    return blk * cos + partner * sin_signed


def _cast_kernel(w_ref, o_ref):
    cols = w_ref.shape[-1]
    o_ref[:, :cols] = w_ref[...].astype(BF16)
    if o_ref.shape[-1] > cols:
        o_ref[:, cols:] = jnp.zeros((o_ref.shape[0], o_ref.shape[-1] - cols), BF16)


def _to_bf16(w, out_cols=None):
    depth, rows, cols = w.shape
    out_cols = out_cols or cols
    tr = 16
    while rows % (2 * tr) == 0 and 2 * tr * cols * 4 <= CAST_BLOCK_BYTES:
        tr *= 2
    return pl.pallas_call(
        _cast_kernel,
        grid=(depth, rows // tr),
        in_specs=[pl.BlockSpec((None, tr, cols), lambda l, i: (l, i, 0))],
        out_specs=pl.BlockSpec((None, tr, out_cols), lambda l, i: (l, i, 0)),
        out_shape=jax.ShapeDtypeStruct((depth, rows, out_cols), BF16),
        compiler_params=pltpu.CompilerParams(
            dimension_semantics=("arbitrary", "arbitrary"), vmem_limit_bytes=VMEM_LIMIT),
        name="cast_bf16",
    )(w)


def _cast_gate_up_kernel(g_ref, u_ref, o_ref):
    tf = g_ref.shape[-1]
    o_ref[:, :tf] = g_ref[...].astype(BF16)
    o_ref[:, tf:] = u_ref[...].astype(BF16)


def _to_bf16_gate_up(w_gu, tf):
    depth, d, f2 = w_gu.shape
    nf = f2 // 2 // tf
    tr = d // 2
    return pl.pallas_call(
        _cast_gate_up_kernel,
        grid=(depth, nf, d // tr),
        in_specs=[pl.BlockSpec((None, tr, tf), lambda l, j, i: (l, i, j)),
                  pl.BlockSpec((None, tr, tf), lambda l, j, i: (l, i, nf + j))],
        out_specs=pl.BlockSpec((None, None, tr, 2 * tf), lambda l, j, i: (l, j, i, 0)),
        out_shape=jax.ShapeDtypeStruct((depth, nf, d, 2 * tf), BF16),
        compiler_params=pltpu.CompilerParams(
            dimension_semantics=("arbitrary", "arbitrary", "arbitrary"),
            vmem_limit_bytes=VMEM_LIMIT),
        name="cast_gate_up",
    )(w_gu, w_gu)


def _mod_kernel(c_ref, w_ref, b_ref, o_ref):
    ca = jax.nn.silu(c_ref[...]).astype(BF16)
    o_ref[...] = jnp.dot(ca, w_ref[...].astype(BF16), preferred_element_type=F32) + b_ref[...]


def _adaln_mod(c_pad, w_ada, b_ada):
    depth, d, n = w_ada.shape
    rows = c_pad.shape[0]
    tn = 1024
    return pl.pallas_call(
        _mod_kernel,
        grid=(depth, n // tn),
        in_specs=[
            pl.BlockSpec((rows, d), lambda l, j: (0, 0)),
            pl.BlockSpec((None, d, tn), lambda l, j: (l, 0, j)),
            pl.BlockSpec((None, 1, tn), lambda l, j: (l, 0, j)),
        ],
        out_specs=pl.BlockSpec((None, rows, tn), lambda l, j: (l, 0, j)),
        out_shape=jax.ShapeDtypeStruct((depth, rows, n), F32),
        compiler_params=pltpu.CompilerParams(
            dimension_semantics=("arbitrary", "arbitrary"), vmem_limit_bytes=VMEM_LIMIT),
        name="adaln_mod",
    )(c_pad, w_ada, b_ada.reshape(depth, 1, n))


def _rope_table_kernel(pos_ref, inv_ref, sign_ref, cos_ref, sin_ref):
    ang = pos_ref[...].astype(F32) * inv_ref[...]
    cos_ref[...] = jnp.cos(ang)
    sin_ref[...] = jnp.sin(ang) * sign_ref[...]


def _rope_tables(positions):
    t = positions.size
    tm = 2048
    inv = ROPE_BASE ** (-jnp.arange(0, ROPE_DIM, 2, dtype=F32) / ROPE_DIM)
    inv_l = jnp.tile(inv, LANES // (ROPE_DIM // 2)).reshape(1, LANES)
    sign = np.where(np.arange(LANES) % ROPE_DIM < ROPE_DIM // 2, -1.0, 1.0).astype(np.float32)
    return pl.pallas_call(
        _rope_table_kernel,
        grid=(t // tm,),
        in_specs=[
            pl.BlockSpec((tm, 1), lambda i: (i, 0)),
            pl.BlockSpec((1, LANES), lambda i: (0, 0)),
            pl.BlockSpec((1, LANES), lambda i: (0, 0)),
        ],
        out_specs=[pl.BlockSpec((tm, LANES), lambda i: (i, 0))] * 2,
        out_shape=[jax.ShapeDtypeStruct((t, LANES), F32)] * 2,
        compiler_params=pltpu.CompilerParams(dimension_semantics=("arbitrary",)),
        name="rope_tables",
    )(positions.reshape(t, 1), inv_l, jnp.asarray(sign).reshape(1, LANES))


def _ffn_kernel(x0_ref, xnext_ref, mod_ref, modnext_ref, gain_ref, wgu_ref, wd_ref, fgain_ref,
                o_ref, h_sc, xres_sc, acc_sc, *, mod_base, final_norm, nchunks):
    i = pl.program_id(0)
    j = pl.program_id(1)
    tf = wd_ref.shape[0]
    chunk = xnext_ref.shape[0]
    slot = i % 2

    def norm_mod(xf, m_ref):
        y = _rms(xf, gain_ref[...])
        return (y * (1 + m_ref[mod_base + 1:mod_base + 2, :]) + m_ref[mod_base:mod_base + 1, :]
                ).astype(BF16)

    @pl.when((i == 0) & (j == 0))
    def _():
        xf = x0_ref[...]
        xres_sc[0] = xf
        h_sc[0] = norm_mod(xf, mod_ref)
        acc_sc[...] = jnp.zeros_like(acc_sc)

    gu = jnp.dot(h_sc[slot], wgu_ref[...], preferred_element_type=F32)
    a = (jax.nn.silu(gu[:, :tf]) * gu[:, tf:]).astype(BF16)
    acc_sc[...] = jnp.where(j > 0, acc_sc[...], 0.0) + jnp.dot(a, wd_ref[...],
                                                               preferred_element_type=F32)

    rows = pl.ds(pl.multiple_of(jnp.minimum(j, nchunks - 1) * chunk, chunk), chunk)
    xc = xnext_ref[...]
    xres_sc[1 - slot, rows, :] = xc
    h_sc[1 - slot, rows, :] = norm_mod(xc, modnext_ref)

    @pl.when(j == pl.num_programs(1) - 1)
    def _():
        gate = mod_ref[mod_base + 2:mod_base + 3, :]
        out = xres_sc[slot] + 0.5 * gate * acc_sc[...]
        if final_norm:
            out = _rms(out, fgain_ref[...])
        o_ref[...] = out


def _ffn(x, mod, gain, w_gu, w_down, final_gain, *, layer, mod_base, final_norm, seq):
    t, d = x.shape
    nf, tf = w_gu.shape[1], w_gu.shape[3] // 2
    tm = FFN_TM
    nt = t // tm
    nchunks = 8
    assert nchunks <= nf and tm % nchunks == 0
    chunk = tm // nchunks
    kern = functools.partial(_ffn_kernel, mod_base=mod_base, final_norm=final_norm, nchunks=nchunks)
    nxt = lambda i: jnp.minimum(i + 1, nt - 1)
    return pl.pallas_call(
        kern,
        grid=(nt, nf),
        in_specs=[
            pl.BlockSpec((tm, d), lambda i, j: (0, 0), pipeline_mode=pl.Buffered(1)),
            pl.BlockSpec((chunk, d), lambda i, j: (nxt(i) * nchunks + jnp.minimum(j, nchunks - 1), 0)),
            pl.BlockSpec((None, None, N_MOD, d), lambda i, j: (layer, i // (seq // tm), 0, 0)),
            pl.BlockSpec((None, None, N_MOD, d), lambda i, j: (layer, nxt(i) // (seq // tm), 0, 0)),
            pl.BlockSpec((1, d), lambda i, j: (0, 0)),
            pl.BlockSpec((None, None, d, 2 * tf), lambda i, j: (layer, j, 0, 0)),
            pl.BlockSpec((None, tf, d), lambda i, j: (layer, j, 0)),
            pl.BlockSpec((1, d), lambda i, j: (0, 0)),
        ],
        out_specs=pl.BlockSpec((tm, d), lambda i, j: (i, 0)),
        out_shape=jax.ShapeDtypeStruct((t, d), F32),
        scratch_shapes=[pltpu.VMEM((2, tm, d), BF16), pltpu.VMEM((2, tm, d), F32),
                        pltpu.VMEM((tm, d), F32)],
        compiler_params=pltpu.CompilerParams(
            dimension_semantics=("arbitrary", "arbitrary"), vmem_limit_bytes=VMEM_LIMIT),
        name="adaln_ffn",
    )(x, x, mod, mod, gain.reshape(1, d), w_gu, w_down, final_gain.reshape(1, d))


def _inproj_kernel(x_ref, mod_ref, gain_ref, cos_ref, sin_ref, win_ref, qg_ref, wuq_ref,
                   kvg_ref, wukv_ref, qk_ref, v_ref, g_ref, qm_ref, km_ref, vm_ref):
    y = _rms(x_ref[...], gain_ref[...])
    h = (y * (1 + mod_ref[4:5, :]) + mod_ref[3:4, :]).astype(BF16)
    cos = cos_ref[...]
    sin = sin_ref[...]

    def proj(c0, c1):
        return jnp.dot(h, win_ref[:, c0:c1], preferred_element_type=F32)

    half = IN_QK // 2
    for c0, mult in ((0, 1.0), (half, RET_DK ** -0.5)):
        p = proj(c0, c0 + half)
        for t in range(half // LANES):
            r = _rope(p[:, t * LANES:(t + 1) * LANES], cos, sin)
            qk_ref[:, c0 + t * LANES:c0 + (t + 1) * LANES] = (r * mult).astype(BF16)
    v_ref[...] = proj(IN_V0, IN_G0).astype(BF16)
    g_ref[...] = proj(IN_G0, IN_CQ0).astype(BF16)

    cq = _rms(proj(IN_CQ0, IN_CKV0), qg_ref[...]).astype(BF16)
    q = jnp.dot(cq, wuq_ref[...], preferred_element_type=F32)
    qscale = (MLA_NOPE + MLA_ROPE) ** -0.5 * LOG2_E
    for hd in range(MLA_HEADS):
        b0 = hd * MLA_QK_PAD
        qm_ref[:, b0:b0 + LANES] = (q[:, b0:b0 + LANES] * qscale).astype(BF16)
        r = _rope(q[:, b0 + LANES:b0 + 2 * LANES], cos, sin)
        qm_ref[:, b0 + LANES:b0 + 2 * LANES] = (r * qscale).astype(BF16)

    tail = proj(IN_CKV0, IN_COLS_PAD)
    ckv = _rms(tail[:, :MLA_KV_RANK], kvg_ref[...]).astype(BF16)
    kpe = _rope(tail[:, MLA_KV_RANK:], cos, sin).astype(BF16)
    kv = jnp.dot(ckv, wukv_ref[...], preferred_element_type=F32)
    for hd in range(MLA_HEADS):
        b0 = hd * (MLA_NOPE + MLA_DV)
        km_ref[:, hd * MLA_QK_PAD:hd * MLA_QK_PAD + LANES] = kv[:, b0:b0 + MLA_NOPE].astype(BF16)
        km_ref[:, hd * MLA_QK_PAD + LANES:(hd + 1) * MLA_QK_PAD] = kpe
        vm_ref[hd * MLA_DV:(hd + 1) * MLA_DV, :] = (
            kv[:, b0 + MLA_NOPE:b0 + MLA_NOPE + MLA_DV].T.astype(BF16))


def _inproj(x, mod, gain, cos, sin, w_in, q_gain, w_uq, kv_gain, w_ukv, *, layer, seq):
    t, d = x.shape
    tm = 256
    const = lambda i: (0, 0)
    row = lambda i: (i, 0)

    def resident(w):
        return pl.BlockSpec((None,) + w.shape[1:], lambda i: (layer, 0, 0),
                            pipeline_mode=pl.Buffered(1))

    widths = [IN_QK, RET_WIDTH, RET_WIDTH, MLA_HEADS * MLA_QK_PAD, MLA_HEADS * MLA_QK_PAD]
    tpb = seq // tm
    sub = ATTN_BLK // tm
    vt_spec = pl.BlockSpec((None, None, MLA_WIDTH, tm),
                           lambda i: (i // tpb, (i % tpb) // sub, 0, i % sub))
    vt_shape = jax.ShapeDtypeStruct((t // seq, seq // ATTN_BLK, MLA_WIDTH, ATTN_BLK), BF16)
    return pl.pallas_call(
        _inproj_kernel,
        grid=(t // tm,),
        in_specs=[
            pl.BlockSpec((tm, d), row),
            pl.BlockSpec((None, None, N_MOD, d), lambda i: (layer, i // (seq // tm), 0, 0)),
            pl.BlockSpec((1, d), const),
            pl.BlockSpec((tm, LANES), row),
            pl.BlockSpec((tm, LANES), row),
            resident(w_in),
            pl.BlockSpec((1, MLA_Q_RANK), const),
            resident(w_uq),
            pl.BlockSpec((1, MLA_KV_RANK), const),
            resident(w_ukv),
        ],
        out_specs=[pl.BlockSpec((tm, w), row) for w in widths] + [vt_spec],
        out_shape=[jax.ShapeDtypeStruct((t, w), BF16) for w in widths] + [vt_shape],
        compiler_params=pltpu.CompilerParams(
            dimension_semantics=("arbitrary",), vmem_limit_bytes=VMEM_LIMIT),
        name="mixer_inproj",
    )(x, mod, gain.reshape(1, d), cos, sin, w_in, q_gain.reshape(1, -1), w_uq,
      kv_gain.reshape(1, -1), w_ukv)


def _retention_kernel(qk_ref, v_ref, g_ref, intra_ref, qdec_ref, kdec_ref, cdec_ref, hmask_ref,
                      gain_ref, o_ref, state_sc, *, chunks):
    @pl.when(pl.program_id(1) == 0)
    def _():
        state_sc[...] = jnp.zeros_like(state_sc)

    kbase = RET_HEADS * RET_DK

    def chunk_body(c, carry):
        rows = pl.ds(pl.multiple_of(c * RET_CHUNK, RET_CHUNK), RET_CHUNK)
        for pair in range(RET_HEADS // 2):
            qf = qk_ref[rows, pair * LANES:(pair + 1) * LANES].astype(F32)
            kp = qk_ref[rows, kbase + pair * LANES:kbase + (pair + 1) * LANES]
            kdt = (kp.astype(F32) * kdec_ref[pair]).T.astype(BF16)
            for hh in range(2):
                hd = 2 * pair + hh
                cols = slice(hd * RET_DV, (hd + 1) * RET_DV)
                vh = v_ref[rows, cols]
                qm = (qf * hmask_ref[hh]).astype(BF16)
                s = lax.dot_general(qm, kp, (((1,), (1,)), ((), ())), preferred_element_type=F32)
                intra = jnp.dot((s * intra_ref[hd]).astype(BF16), vh, preferred_element_type=F32)
                qd = (qf * qdec_ref[hd]).astype(BF16)
                state = state_sc[hd]
                cross = jnp.dot(qd, state.astype(BF16), preferred_element_type=F32)
                state_sc[hd] = state * cdec_ref[hd] + jnp.dot(kdt, vh, preferred_element_type=F32)
                yh = _rms(intra + cross, gain_ref[:, cols])
                o_ref[rows, cols] = (jax.nn.silu(g_ref[rows, cols].astype(F32)) * yh).astype(BF16)
        return carry

    lax.fori_loop(0, chunks, chunk_body, 0)


def _retention_consts():
    c = RET_CHUNK
    log_g = jnp.log1p(-(2.0 ** (-5.0 - jnp.arange(RET_HEADS, dtype=F32))))
    idx = jnp.arange(c, dtype=F32)
    rel = idx[:, None] - idx[None, :]
    intra = jnp.where(rel >= 0, jnp.exp(log_g[:, None, None] * jnp.maximum(rel, 0.0)), 0.0)
    q_decay = jnp.exp(log_g[:, None] * (idx + 1.0))
    k_decay = jnp.exp(log_g[:, None] * (c - 1.0 - idx))
    chunk_decay = jnp.exp(log_g * c)
    own = (np.arange(LANES)[None, :] // RET_DK == np.arange(2)[:, None]).astype(np.float32)
    hmask = jnp.asarray(own).reshape(2, 1, LANES)
    qdec = q_decay[:, :, None] * jnp.asarray(own)[jnp.arange(RET_HEADS) % 2][:, None, :]
    kdec = jnp.repeat(k_decay.reshape(RET_HEADS // 2, 2, c), RET_DK, axis=1).transpose(0, 2, 1)
    cdec = jnp.broadcast_to(chunk_decay[:, None, None], (RET_HEADS, 1, LANES))
    return intra, qdec, kdec, cdec, hmask


def _retention(qk, v, g, gain, *, batch, seq):
    tb = 1024
    chunks = tb // RET_CHUNK
    nb = seq // tb
    intra, qdec, kdec, cdec, hmask = _retention_consts()
    row = lambda b, i: (b * nb + i, 0)
    c3 = lambda b, i: (0, 0, 0)
    return pl.pallas_call(
        functools.partial(_retention_kernel, chunks=chunks),
        grid=(batch, nb),
        in_specs=[
            pl.BlockSpec((tb, IN_QK), row),
            pl.BlockSpec((tb, RET_WIDTH), row),
            pl.BlockSpec((tb, RET_WIDTH), row),
            pl.BlockSpec(intra.shape, c3),
            pl.BlockSpec(qdec.shape, c3),
            pl.BlockSpec(kdec.shape, c3),
            pl.BlockSpec(cdec.shape, c3),
            pl.BlockSpec(hmask.shape, c3),
            pl.BlockSpec((1, RET_WIDTH), lambda b, i: (0, 0)),
        ],
        out_specs=pl.BlockSpec((tb, RET_WIDTH), row),
        out_shape=jax.ShapeDtypeStruct((batch * seq, RET_WIDTH), BF16),
        scratch_shapes=[pltpu.VMEM((RET_HEADS, LANES, RET_DV), F32)],
        compiler_params=pltpu.CompilerParams(
            dimension_semantics=("arbitrary", "arbitrary"), vmem_limit_bytes=VMEM_LIMIT),
        name="retention",
    )(qk, v, g, intra, qdec, kdec, cdec, hmask, gain.reshape(1, RET_WIDTH))


def _attn_kernel(q_ref, qnext_ref, k_ref, vt_ref, o_ref, s_sc, acc_sc, m_sc, l_sc):
    qi = pl.program_id(2)
    blk = ATTN_BLK
    acc_sc[...] = jnp.zeros_like(acc_sc)

    def scores(ki, slot, queries=q_ref):
        rows = pl.ds(pl.multiple_of(ki * blk, blk), blk)
        for hd in range(ATTN_HEADS):
            cols = slice(hd * MLA_QK_PAD, (hd + 1) * MLA_QK_PAD)
            s_sc[slot, hd] = lax.dot_general(k_ref[rows, cols], queries[:, cols],
                                             (((1,), (1,)), ((), ())), preferred_element_type=F32)

    def update(ki, slot, stats, masked):
        new_stats = []
        for hd in range(ATTN_HEADS):
            m_prev, l_prev = stats[hd]
            s = s_sc[slot, hd]
            if masked:
                kpos = lax.broadcasted_iota(jnp.int32, s.shape, 0)
                qpos = lax.broadcasted_iota(jnp.int32, s.shape, 1)
                s = jnp.where(kpos <= qpos, s, -jnp.inf)
            m_new = jnp.maximum(m_prev, jnp.max(s, axis=0, keepdims=True))
            p = jnp.exp2(s - m_new)
            alpha = jnp.exp2(m_prev - m_new)
            l_new = alpha * l_prev + jnp.sum(p, axis=0, keepdims=True)
            vt = vt_ref[ki, hd * MLA_DV:(hd + 1) * MLA_DV, :]
            acc_sc[hd] = alpha * acc_sc[hd] + jnp.dot(vt, p.astype(BF16), preferred_element_type=F32)
            new_stats.append((m_new, l_new))
        return tuple(new_stats)

    def load_stats():
        return tuple((m_sc[hd], l_sc[hd]) for hd in range(ATTN_HEADS))

    def store_stats(stats):
        for hd in range(ATTN_HEADS):
            m_sc[hd], l_sc[hd] = stats[hd]

    @pl.when(qi == 0)
    def _():
        scores(0, 0)

    def block_pair(kp, stats):
        scores(2 * kp + 1, 1)
        stats = update(2 * kp, 0, stats, False)
        scores(2 * kp + 2, 0)
        stats = update(2 * kp + 1, 1, stats, False)
        return stats

    init = tuple((jnp.full((1, blk), -jnp.inf, F32), jnp.zeros((1, blk), F32))
                 for _ in range(ATTN_HEADS))
    store_stats(lax.fori_loop(0, qi // 2, block_pair, init))

    @pl.when(qi % 2 == 1)
    def _():
        stats = update(qi - 1, 0, load_stats(), False)
        scores(qi, 1)
        store_stats(stats)

    stats = update(qi, qi % 2, load_stats(), True)
    scores(0, 0, qnext_ref)
    for hd in range(ATTN_HEADS):
        out_t = acc_sc[hd] / stats[hd][1]
        o_ref[:, hd * MLA_DV:(hd + 1) * MLA_DV] = out_t.T.astype(BF16)


def _attention(qm, km, vt, *, batch, seq):
    blk = ATTN_BLK
    nq = seq // blk
    qm = qm.reshape(batch, seq, -1)
    km = km.reshape(batch, seq, -1)
    out = pl.pallas_call(
        _attn_kernel,
        grid=(batch, MLA_HEADS // ATTN_HEADS, nq),
        in_specs=[
            pl.BlockSpec((None, blk, ATTN_HEADS * MLA_QK_PAD), lambda b, h, i: (b, i, h)),
            pl.BlockSpec((None, blk, ATTN_HEADS * MLA_QK_PAD),
                         lambda b, h, i: (b, jnp.minimum(i + 1, nq - 1), h)),
            pl.BlockSpec((None, seq, ATTN_HEADS * MLA_QK_PAD), lambda b, h, i: (b, 0, h)),
            pl.BlockSpec((None, nq, ATTN_HEADS * MLA_DV, blk), lambda b, h, i: (b, 0, h, 0)),
        ],
        out_specs=pl.BlockSpec((None, blk, ATTN_HEADS * MLA_DV), lambda b, h, i: (b, i, h)),
        out_shape=jax.ShapeDtypeStruct((batch, seq, MLA_WIDTH), BF16),
        scratch_shapes=[pltpu.VMEM((2, ATTN_HEADS, blk, blk), F32),
                        pltpu.VMEM((ATTN_HEADS, MLA_DV, blk), F32),
                        pltpu.VMEM((ATTN_HEADS, 1, blk), F32), pltpu.VMEM((ATTN_HEADS, 1, blk), F32)],
        compiler_params=pltpu.CompilerParams(
            dimension_semantics=("arbitrary", "arbitrary", "arbitrary"),
            vmem_limit_bytes=VMEM_LIMIT),
        name="mla_attention",
    )(qm, qm, km, vt)
    return out.reshape(batch * seq, MLA_WIDTH)


def _outproj_kernel(x_ref, mod_ref, yr_ref, ym_ref, w_ref, o_ref):
    y = jnp.dot(yr_ref[...], w_ref[:RET_WIDTH, :], preferred_element_type=F32)
    y += jnp.dot(ym_ref[...], w_ref[RET_WIDTH:, :], preferred_element_type=F32)
    o_ref[...] = x_ref[...] + mod_ref[5:6, :] * y


def _outproj(x, mod, y_r, y_m, w_out, *, layer, seq):
    t, d = x.shape
    tm = 512
    row = lambda i: (i, 0)
    return pl.pallas_call(
        _outproj_kernel,
        grid=(t // tm,),
        in_specs=[
            pl.BlockSpec((tm, d), row),
            pl.BlockSpec((None, None, N_MOD, d), lambda i: (layer, i // (seq // tm), 0, 0)),
            pl.BlockSpec((tm, RET_WIDTH), row),
            pl.BlockSpec((tm, MLA_WIDTH), row),
            pl.BlockSpec((None,) + w_out.shape[1:], lambda i: (layer, 0, 0),
                         pipeline_mode=pl.Buffered(1)),
        ],
        out_specs=pl.BlockSpec((tm, d), row),
        out_shape=jax.ShapeDtypeStruct((t, d), F32),
        compiler_params=pltpu.CompilerParams(
            dimension_semantics=("arbitrary",), vmem_limit_bytes=VMEM_LIMIT),
        name="mixer_outproj",
    )(x, mod, y_r, y_m, w_out)


def kernel(x, c, positions, w_ada, b_ada, norm_ffn1, ffn1_w_gu, ffn1_w_down, norm_mix, w_in,
           ret_norm_g, q_norm_g, w_uq, kv_norm_g, w_ukv, w_out, norm_ffn2, ffn2_w_gu,
           ffn2_w_down, final_norm):
    batch, seq, d = x.shape
    depth = w_ada.shape[0]
    xt = x.reshape(batch * seq, d)

    c_pad = jnp.pad(c, ((0, 8 - batch), (0, 0)))
    mod = _adaln_mod(c_pad, w_ada, b_ada)[:, :batch].reshape(depth, batch, N_MOD, d)
    cos, sin = _rope_tables(positions)

    w_in_b = jnp.pad(w_in, ((0, 0), (0, 0), (0, IN_COLS_PAD - w_in.shape[-1]))).astype(BF16)
    w_uq_b = jnp.pad(w_uq.reshape(depth, MLA_Q_RANK, MLA_HEADS, MLA_NOPE + MLA_ROPE),
                     ((0, 0), (0, 0), (0, 0), (0, MLA_QK_PAD - MLA_NOPE - MLA_ROPE))
                     ).reshape(depth, MLA_Q_RANK, MLA_HEADS * MLA_QK_PAD).astype(BF16)
    w_ukv_b = _to_bf16(w_ukv)
    w_out_b = _to_bf16(w_out)
    ffn_w = [(_to_bf16_gate_up(ffn1_w_gu, FFN_TF), _to_bf16(ffn1_w_down)),
             (_to_bf16_gate_up(ffn2_w_gu, FFN_TF), _to_bf16(ffn2_w_down))]
    ffn_gain = [norm_ffn1, norm_ffn2]

    def ffn(xt, l, which):
        w_gu, w_down = ffn_w[which]
        return _ffn(xt, mod, ffn_gain[which][l], w_gu, w_down, final_norm, layer=l,
                    mod_base=6 * which, final_norm=(which == 1 and l == depth - 1), seq=seq)

    for l in range(depth):
        xt = ffn(xt, l, 0)
        qk, v_r, g_r, qm, km, vt = _inproj(xt, mod, norm_mix[l], cos, sin, w_in_b, q_norm_g[l],
                                           w_uq_b, kv_norm_g[l], w_ukv_b, layer=l, seq=seq)
        y_r = _retention(qk, v_r, g_r, ret_norm_g[l], batch=batch, seq=seq)
        y_m = _attention(qm, km, vt, batch=batch, seq=seq)
        xt = _outproj(xt, mod, y_r, y_m, w_out_b, layer=l, seq=seq)
        xt = ffn(xt, l, 1)
    return xt.reshape(batch, seq, d)
```

```python
import functools

import jax
import jax.numpy as jnp
import numpy as np
from jax import lax
from jax.experimental import pallas as pl
from jax.experimental.pallas import tpu as pltpu

D_MODEL = 2048
DEPTH = 2
RET_HEADS = 8
RET_DK = 64
RET_DV = 128
RET_CHUNK = 128
MLA_HEADS = 8
MLA_Q_RANK = 512
MLA_KV_RANK = 256
MLA_NOPE = 128
MLA_ROPE = 64
MLA_DV = 128
RET_WIDTH = RET_HEADS * RET_DV
MLA_WIDTH = MLA_HEADS * MLA_DV
ROPE_DIM = 64
ROPE_BASE = 10000.0
EPS = 1e-6
N_MOD = 9

LANES = 128
MLA_QK_PAD = 256
IN_QK = 2 * RET_HEADS * RET_DK
IN_V0 = IN_QK
IN_G0 = IN_V0 + RET_WIDTH
IN_CQ0 = IN_G0 + RET_WIDTH
IN_CKV0 = IN_CQ0 + MLA_Q_RANK
IN_KPE0 = IN_CKV0 + MLA_KV_RANK
IN_COLS_PAD = IN_KPE0 + LANES

VMEM_LIMIT = 56 * 1024 * 1024
ATTN_BLK = 512
ATTN_HEADS = 4
LOG2_E = 1.4426950408889634
CAST_BLOCK_BYTES = 8 * 1024 * 1024
FFN_TM = 512
FFN_TF = 512
BF16_SUBLANES = 16

BF16 = jnp.bfloat16
F32 = jnp.float32


def _rms(xf, gain):
    return xf * lax.rsqrt(jnp.mean(xf * xf, axis=-1, keepdims=True) + EPS) * gain


def _rope(blk, cos, sin_signed):
    lane = lax.broadcasted_iota(jnp.int32, blk.shape, 1)
    partner = jnp.where(lane % ROPE_DIM < ROPE_DIM // 2,
                        pltpu.roll(blk, LANES - ROPE_DIM // 2, 1),
                        pltpu.roll(blk, ROPE_DIM // 2, 1))
    return blk * cos + partner * sin_signed


def _cast_kernel(w_ref, o_ref):
    cols = w_ref.shape[-1]
    o_ref[:, :cols] = w_ref[...].astype(BF16)
    if o_ref.shape[-1] > cols:
        o_ref[:, cols:] = jnp.zeros((o_ref.shape[0], o_ref.shape[-1] - cols), BF16)


def _to_bf16(w, out_cols=None, layer=None):
    depth, rows, cols = w.shape
    out_cols = out_cols or cols
    tr = 16
    while rows % (2 * tr) == 0 and 2 * tr * cols * 4 <= CAST_BLOCK_BYTES:
        tr *= 2
    if layer is None:
        grid = (depth, rows // tr)
        in_spec = pl.BlockSpec((None, tr, cols), lambda l, i: (l, i, 0))
        out_spec = pl.BlockSpec((None, tr, out_cols), lambda l, i: (l, i, 0))
        out_shape = (depth, rows, out_cols)
    else:
        grid = (rows // tr,)
        in_spec = pl.BlockSpec((None, tr, cols), lambda i: (layer, i, 0))
        out_spec = pl.BlockSpec((tr, out_cols), lambda i: (i, 0))
        out_shape = (rows, out_cols)
    return pl.pallas_call(
        _cast_kernel,
        grid=grid,
        in_specs=[in_spec],
        out_specs=out_spec,
        out_shape=jax.ShapeDtypeStruct(out_shape, BF16),
        compiler_params=pltpu.CompilerParams(
            dimension_semantics=("arbitrary",) * len(grid), vmem_limit_bytes=VMEM_LIMIT),
        name="cast_bf16",
    )(w)


def _mod_kernel(c_ref, w_ref, b_ref, o_ref):
    ca = jax.nn.silu(c_ref[...]).astype(BF16)
    o_ref[...] = jnp.dot(ca, w_ref[...].astype(BF16), preferred_element_type=F32) + b_ref[...]


def _adaln_mod(c_pad, w_ada, b_ada):
    depth, d, n = w_ada.shape
    rows = c_pad.shape[0]
    tn = 1024
    return pl.pallas_call(
        _mod_kernel,
        grid=(depth, n // tn),
        in_specs=[
            pl.BlockSpec((rows, d), lambda l, j: (0, 0)),
            pl.BlockSpec((None, d, tn), lambda l, j: (l, 0, j)),
            pl.BlockSpec((None, 1, tn), lambda l, j: (l, 0, j)),
        ],
        out_specs=pl.BlockSpec((None, rows, tn), lambda l, j: (l, 0, j)),
        out_shape=jax.ShapeDtypeStruct((depth, rows, n), F32),
        compiler_params=pltpu.CompilerParams(
            dimension_semantics=("arbitrary", "arbitrary"), vmem_limit_bytes=VMEM_LIMIT),
        name="adaln_mod",
    )(c_pad, w_ada, b_ada.reshape(depth, 1, n))


def _rope_table_kernel(pos_ref, inv_ref, sign_ref, cos_ref, sin_ref):
    ang = pos_ref[...].astype(F32) * inv_ref[...]
    cos_ref[...] = jnp.cos(ang)
    sin_ref[...] = jnp.sin(ang) * sign_ref[...]


def _rope_tables(positions):
    t = positions.size
    tm = 2048
    inv = ROPE_BASE ** (-jnp.arange(0, ROPE_DIM, 2, dtype=F32) / ROPE_DIM)
    inv_l = jnp.tile(inv, LANES // (ROPE_DIM // 2)).reshape(1, LANES)
    sign = np.where(np.arange(LANES) % ROPE_DIM < ROPE_DIM // 2, -1.0, 1.0).astype(np.float32)
    return pl.pallas_call(
        _rope_table_kernel,
        grid=(t // tm,),
        in_specs=[
            pl.BlockSpec((tm, 1), lambda i: (i, 0)),
            pl.BlockSpec((1, LANES), lambda i: (0, 0)),
            pl.BlockSpec((1, LANES), lambda i: (0, 0)),
        ],
        out_specs=[pl.BlockSpec((tm, LANES), lambda i: (i, 0))] * 2,
        out_shape=[jax.ShapeDtypeStruct((t, LANES), F32)] * 2,
        compiler_params=pltpu.CompilerParams(dimension_semantics=("arbitrary",)),
        name="rope_tables",
    )(positions.reshape(t, 1), inv_l, jnp.asarray(sign).reshape(1, LANES))


def _ffn_kernel(*refs, mod_base, final_norm, nchunks, side_cast):
    (x0_ref, xnext_ref, mod_ref, modnext_ref, gain_ref, wg_ref, wu_ref, wd_ref, fgain_ref
     ) = refs[:9]
    if side_cast:
        side_gu_ref, side_d_ref, o_ref, side_gu_out, side_d_out, h_sc, xres_sc, acc_sc = refs[9:]
    else:
        o_ref, h_sc, xres_sc, acc_sc = refs[9:]
    i = pl.program_id(0)
    j = pl.program_id(1)
    chunk = xnext_ref.shape[0]
    slot = i % 2

    def norm_mod(xf, m_ref):
        y = _rms(xf, gain_ref[...])
        return (y * (1 + m_ref[mod_base + 1:mod_base + 2, :]) + m_ref[mod_base:mod_base + 1, :]
                ).astype(BF16)

    @pl.when((i == 0) & (j == 0))
    def _():
        xf = x0_ref[...]
        xres_sc[0] = xf
        h_sc[0] = norm_mod(xf, mod_ref)
        acc_sc[...] = jnp.zeros_like(acc_sc)

    h = h_sc[slot]
    g = jnp.dot(h, wg_ref[...], preferred_element_type=F32)
    u = jnp.dot(h, wu_ref[...], preferred_element_type=F32)
    a = (jax.nn.silu(g) * u).astype(BF16)
    acc_sc[...] = jnp.where(j > 0, acc_sc[...], 0.0) + jnp.dot(a, wd_ref[...],
                                                               preferred_element_type=F32)

    rows = pl.ds(pl.multiple_of(jnp.minimum(j, nchunks - 1) * chunk, chunk), chunk)
    xc = xnext_ref[...]
    xres_sc[1 - slot, rows, :] = xc
    h_sc[1 - slot, rows, :] = norm_mod(xc, modnext_ref)

    if side_cast:
        side_gu_out[...] = side_gu_ref[...].astype(BF16)
        side_d_out[...] = side_d_ref[...].astype(BF16)

    @pl.when(j == pl.num_programs(1) - 1)
    def _():
        gate = mod_ref[mod_base + 2:mod_base + 3, :]
        out = xres_sc[slot] + 0.5 * gate * acc_sc[...]
        if final_norm:
            out = _rms(out, fgain_ref[...])
        o_ref[...] = out


def _ffn(x, mod, gain, w_gu, w_down, final_gain, side, *, layer, mod_base, final_norm, seq):
    t, d = x.shape
    f = w_down.shape[0]
    tm, tf = FFN_TM, FFN_TF
    nf = f // tf
    nt = t // tm
    nchunks = 8
    assert nchunks <= nf and tm % nchunks == 0
    chunk = tm // nchunks
    kern = functools.partial(_ffn_kernel, mod_base=mod_base, final_norm=final_norm,
                             nchunks=nchunks, side_cast=side is not None)
    nxt = lambda i: jnp.minimum(i + 1, nt - 1)
    in_specs = [
        pl.BlockSpec((tm, d), lambda i, j: (0, 0), pipeline_mode=pl.Buffered(1)),
        pl.BlockSpec((chunk, d), lambda i, j: (nxt(i) * nchunks + jnp.minimum(j, nchunks - 1), 0)),
        pl.BlockSpec((None, None, N_MOD, d), lambda i, j: (layer, i // (seq // tm), 0, 0)),
        pl.BlockSpec((None, None, N_MOD, d), lambda i, j: (layer, nxt(i) // (seq // tm), 0, 0)),
        pl.BlockSpec((1, d), lambda i, j: (0, 0)),
        pl.BlockSpec((d, tf), lambda i, j: (0, j)),
        pl.BlockSpec((d, tf), lambda i, j: (0, nf + j)),
        pl.BlockSpec((tf, d), lambda i, j: (j, 0)),
        pl.BlockSpec((1, d), lambda i, j: (0, 0)),
    ]
    operands = [x, x, mod, mod, gain.reshape(1, d), w_gu, w_gu, w_down, final_gain.reshape(1, d)]
    out_specs = [pl.BlockSpec((tm, d), lambda i, j: (i, 0))]
    out_shape = [jax.ShapeDtypeStruct((t, d), F32)]
    if side is not None:
        side_layer = side[2]
        for w in side[:2]:
            _, rows, cols = w.shape
            rps = -(-rows // (BF16_SUBLANES * nt * nf)) * BF16_SUBLANES
            assert rows % rps == 0
            blk = lambda i, j, last=rows // rps - 1: jnp.minimum(i * nf + j, last)
            in_specs.append(pl.BlockSpec((None, rps, cols),
                                         lambda i, j, blk=blk: (side_layer, blk(i, j), 0)))
            operands.append(w)
            out_specs.append(pl.BlockSpec((rps, cols), lambda i, j, blk=blk: (blk(i, j), 0)))
            out_shape.append(jax.ShapeDtypeStruct((rows, cols), BF16))
    outs = pl.pallas_call(
        kern,
        grid=(nt, nf),
        in_specs=in_specs,
        out_specs=out_specs,
        out_shape=out_shape,
        scratch_shapes=[pltpu.VMEM((2, tm, d), BF16), pltpu.VMEM((2, tm, d), F32),
                        pltpu.VMEM((tm, d), F32)],
        compiler_params=pltpu.CompilerParams(
            dimension_semantics=("arbitrary", "arbitrary"), vmem_limit_bytes=VMEM_LIMIT),
        name="adaln_ffn",
    )(*operands)
    return outs[0], tuple(outs[1:])


def _inproj_kernel(x_ref, mod_ref, gain_ref, cos_ref, sin_ref, win_ref, qg_ref, wuq_ref,
                   kvg_ref, wukv_ref, qk_ref, v_ref, g_ref, qm_ref, km_ref, vm_ref):
    y = _rms(x_ref[...], gain_ref[...])
    h = (y * (1 + mod_ref[4:5, :]) + mod_ref[3:4, :]).astype(BF16)
    cos = cos_ref[...]
    sin = sin_ref[...]

    def proj(c0, c1):
        return jnp.dot(h, win_ref[:, c0:c1], preferred_element_type=F32)

    half = IN_QK // 2
    for c0, mult in ((0, 1.0), (half, RET_DK ** -0.5)):
        p = proj(c0, c0 + half)
        for t in range(half // LANES):
            r = _rope(p[:, t * LANES:(t + 1) * LANES], cos, sin)
            qk_ref[:, c0 + t * LANES:c0 + (t + 1) * LANES] = (r * mult).astype(BF16)
    v_ref[...] = proj(IN_V0, IN_G0).astype(BF16)
    g_ref[...] = proj(IN_G0, IN_CQ0).astype(BF16)

    cq = _rms(proj(IN_CQ0, IN_CKV0), qg_ref[...]).astype(BF16)
    q = jnp.dot(cq, wuq_ref[...], preferred_element_type=F32)
    qscale = (MLA_NOPE + MLA_ROPE) ** -0.5 * LOG2_E
    for hd in range(MLA_HEADS):
        b0 = hd * MLA_QK_PAD
        qm_ref[:, b0:b0 + LANES] = (q[:, b0:b0 + LANES] * qscale).astype(BF16)
        r = _rope(q[:, b0 + LANES:b0 + 2 * LANES], cos, sin)
        qm_ref[:, b0 + LANES:b0 + 2 * LANES] = (r * qscale).astype(BF16)

    tail = proj(IN_CKV0, IN_COLS_PAD)
    ckv = _rms(tail[:, :MLA_KV_RANK], kvg_ref[...]).astype(BF16)
    kpe = _rope(tail[:, MLA_KV_RANK:], cos, sin).astype(BF16)
    kv = jnp.dot(ckv, wukv_ref[...], preferred_element_type=F32)
    for hd in range(MLA_HEADS):
        b0 = hd * (MLA_NOPE + MLA_DV)
        km_ref[:, hd * MLA_QK_PAD:hd * MLA_QK_PAD + LANES] = kv[:, b0:b0 + MLA_NOPE].astype(BF16)
        km_ref[:, hd * MLA_QK_PAD + LANES:(hd + 1) * MLA_QK_PAD] = kpe
        vm_ref[hd * MLA_DV:(hd + 1) * MLA_DV, :] = (
            kv[:, b0 + MLA_NOPE:b0 + MLA_NOPE + MLA_DV].T.astype(BF16))


def _inproj(x, mod, gain, cos, sin, w_in, q_gain, w_uq, kv_gain, w_ukv, *, layer, seq):
    t, d = x.shape
    tm = 256
    const = lambda i: (0, 0)
    row = lambda i: (i, 0)

    def resident(w):
        return pl.BlockSpec((None,) + w.shape[1:], lambda i: (layer, 0, 0),
                            pipeline_mode=pl.Buffered(1))

    widths = [IN_QK, RET_WIDTH, RET_WIDTH, MLA_HEADS * MLA_QK_PAD, MLA_HEADS * MLA_QK_PAD]
    tpb = seq // tm
    sub = ATTN_BLK // tm
    vt_spec = pl.BlockSpec((None, None, MLA_WIDTH, tm),
                           lambda i: (i // tpb, (i % tpb) // sub, 0, i % sub))
    vt_shape = jax.ShapeDtypeStruct((t // seq, seq // ATTN_BLK, MLA_WIDTH, ATTN_BLK), BF16)
    return pl.pallas_call(
        _inproj_kernel,
        grid=(t // tm,),
        in_specs=[
            pl.BlockSpec((tm, d), row),
            pl.BlockSpec((None, None, N_MOD, d), lambda i: (layer, i // (seq // tm), 0, 0)),
            pl.BlockSpec((1, d), const),
            pl.BlockSpec((tm, LANES), row),
            pl.BlockSpec((tm, LANES), row),
            resident(w_in),
            pl.BlockSpec((1, MLA_Q_RANK), const),
            resident(w_uq),
            pl.BlockSpec((1, MLA_KV_RANK), const),
            resident(w_ukv),
        ],
        out_specs=[pl.BlockSpec((tm, w), row) for w in widths] + [vt_spec],
        out_shape=[jax.ShapeDtypeStruct((t, w), BF16) for w in widths] + [vt_shape],
        compiler_params=pltpu.CompilerParams(
            dimension_semantics=("arbitrary",), vmem_limit_bytes=VMEM_LIMIT),
        name="mixer_inproj",
    )(x, mod, gain.reshape(1, d), cos, sin, w_in, q_gain.reshape(1, -1), w_uq,
      kv_gain.reshape(1, -1), w_ukv)


def _retention_kernel(qk_ref, v_ref, g_ref, intra_ref, qdec_ref, kdec_ref, cdec_ref, hmask_ref,
                      gain_ref, o_ref, state_sc, *, chunks):
    @pl.when(pl.program_id(1) == 0)
    def _():
        state_sc[...] = jnp.zeros_like(state_sc)

    kbase = RET_HEADS * RET_DK

    def chunk_body(c, carry):
        rows = pl.ds(pl.multiple_of(c * RET_CHUNK, RET_CHUNK), RET_CHUNK)
        for pair in range(RET_HEADS // 2):
            qf = qk_ref[rows, pair * LANES:(pair + 1) * LANES].astype(F32)
            kp = qk_ref[rows, kbase + pair * LANES:kbase + (pair + 1) * LANES]
            kdt = (kp.astype(F32) * kdec_ref[pair]).T.astype(BF16)
            for hh in range(2):
                hd = 2 * pair + hh
                cols = slice(hd * RET_DV, (hd + 1) * RET_DV)
                vh = v_ref[rows, cols]
                qm = (qf * hmask_ref[hh]).astype(BF16)
                s = lax.dot_general(qm, kp, (((1,), (1,)), ((), ())), preferred_element_type=F32)
                intra = jnp.dot((s * intra_ref[hd]).astype(BF16), vh, preferred_element_type=F32)
                qd = (qf * qdec_ref[hd]).astype(BF16)
                state = state_sc[hd]
                cross = jnp.dot(qd, state.astype(BF16), preferred_element_type=F32)
                state_sc[hd] = state * cdec_ref[hd] + jnp.dot(kdt, vh, preferred_element_type=F32)
                yh = _rms(intra + cross, gain_ref[:, cols])
                o_ref[rows, cols] = (jax.nn.silu(g_ref[rows, cols].astype(F32)) * yh).astype(BF16)
        return carry

    lax.fori_loop(0, chunks, chunk_body, 0)


def _retention_consts():
    c = RET_CHUNK
    log_g = jnp.log1p(-(2.0 ** (-5.0 - jnp.arange(RET_HEADS, dtype=F32))))
    idx = jnp.arange(c, dtype=F32)
    rel = idx[:, None] - idx[None, :]
    intra = jnp.where(rel >= 0, jnp.exp(log_g[:, None, None] * jnp.maximum(rel, 0.0)), 0.0)
    q_decay = jnp.exp(log_g[:, None] * (idx + 1.0))
    k_decay = jnp.exp(log_g[:, None] * (c - 1.0 - idx))
    chunk_decay = jnp.exp(log_g * c)
    own = (np.arange(LANES)[None, :] // RET_DK == np.arange(2)[:, None]).astype(np.float32)
    hmask = jnp.asarray(own).reshape(2, 1, LANES)
    qdec = q_decay[:, :, None] * jnp.asarray(own)[jnp.arange(RET_HEADS) % 2][:, None, :]
    kdec = jnp.repeat(k_decay.reshape(RET_HEADS // 2, 2, c), RET_DK, axis=1).transpose(0, 2, 1)
    cdec = jnp.broadcast_to(chunk_decay[:, None, None], (RET_HEADS, 1, LANES))
    return intra, qdec, kdec, cdec, hmask


def _retention(qk, v, g, gain, *, batch, seq):
    tb = 1024
    chunks = tb // RET_CHUNK
    nb = seq // tb
    intra, qdec, kdec, cdec, hmask = _retention_consts()
    row = lambda b, i: (b * nb + i, 0)
    c3 = lambda b, i: (0, 0, 0)
    return pl.pallas_call(
        functools.partial(_retention_kernel, chunks=chunks),
        grid=(batch, nb),
        in_specs=[
            pl.BlockSpec((tb, IN_QK), row),
            pl.BlockSpec((tb, RET_WIDTH), row),
            pl.BlockSpec((tb, RET_WIDTH), row),
            pl.BlockSpec(intra.shape, c3),
            pl.BlockSpec(qdec.shape, c3),
            pl.BlockSpec(kdec.shape, c3),
            pl.BlockSpec(cdec.shape, c3),
            pl.BlockSpec(hmask.shape, c3),
            pl.BlockSpec((1, RET_WIDTH), lambda b, i: (0, 0)),
        ],
        out_specs=pl.BlockSpec((tb, RET_WIDTH), row),
        out_shape=jax.ShapeDtypeStruct((batch * seq, RET_WIDTH), BF16),
        scratch_shapes=[pltpu.VMEM((RET_HEADS, LANES, RET_DV), F32)],
        compiler_params=pltpu.CompilerParams(
            dimension_semantics=("arbitrary", "arbitrary"), vmem_limit_bytes=VMEM_LIMIT),
        name="retention",
    )(qk, v, g, intra, qdec, kdec, cdec, hmask, gain.reshape(1, RET_WIDTH))


def _attn_kernel(q_ref, qnext_ref, k_ref, vt_ref, o_ref, s_sc, acc_sc, m_sc, l_sc):
    qi = pl.program_id(2)
    blk = ATTN_BLK
    acc_sc[...] = jnp.zeros_like(acc_sc)

    def scores(ki, slot, queries=q_ref):
        rows = pl.ds(pl.multiple_of(ki * blk, blk), blk)
        for hd in range(ATTN_HEADS):
            cols = slice(hd * MLA_QK_PAD, (hd + 1) * MLA_QK_PAD)
            s_sc[slot, hd] = lax.dot_general(k_ref[rows, cols], queries[:, cols],
                                             (((1,), (1,)), ((), ())), preferred_element_type=F32)

    def update(ki, slot, stats, masked):
        new_stats = []
        for hd in range(ATTN_HEADS):
            m_prev, l_prev = stats[hd]
            s = s_sc[slot, hd]
            if masked:
                kpos = lax.broadcasted_iota(jnp.int32, s.shape, 0)
                qpos = lax.broadcasted_iota(jnp.int32, s.shape, 1)
                s = jnp.where(kpos <= qpos, s, -jnp.inf)
            m_new = jnp.maximum(m_prev, jnp.max(s, axis=0, keepdims=True))
            p = jnp.exp2(s - m_new)
            alpha = jnp.exp2(m_prev - m_new)
            l_new = alpha * l_prev + jnp.sum(p, axis=0, keepdims=True)
            vt = vt_ref[ki, hd * MLA_DV:(hd + 1) * MLA_DV, :]
            acc_sc[hd] = alpha * acc_sc[hd] + jnp.dot(vt, p.astype(BF16), preferred_element_type=F32)
            new_stats.append((m_new, l_new))
        return tuple(new_stats)

    def load_stats():
        return tuple((m_sc[hd], l_sc[hd]) for hd in range(ATTN_HEADS))

    def store_stats(stats):
        for hd in range(ATTN_HEADS):
            m_sc[hd], l_sc[hd] = stats[hd]

    @pl.when(qi == 0)
    def _():
        scores(0, 0)

    def block_pair(kp, stats):
        scores(2 * kp + 1, 1)
        stats = update(2 * kp, 0, stats, False)
        scores(2 * kp + 2, 0)
        stats = update(2 * kp + 1, 1, stats, False)
        return stats

    init = tuple((jnp.full((1, blk), -jnp.inf, F32), jnp.zeros((1, blk), F32))
                 for _ in range(ATTN_HEADS))
    store_stats(lax.fori_loop(0, qi // 2, block_pair, init))

    @pl.when(qi % 2 == 1)
    def _():
        stats = update(qi - 1, 0, load_stats(), False)
        scores(qi, 1)
        store_stats(stats)

    stats = update(qi, qi % 2, load_stats(), True)
    scores(0, 0, qnext_ref)
    for hd in range(ATTN_HEADS):
        out_t = acc_sc[hd] / stats[hd][1]
        o_ref[:, hd * MLA_DV:(hd + 1) * MLA_DV] = out_t.T.astype(BF16)


def _attention(qm, km, vt, *, batch, seq):
    blk = ATTN_BLK
    nq = seq // blk
    qm = qm.reshape(batch, seq, -1)
    km = km.reshape(batch, seq, -1)
    out = pl.pallas_call(
        _attn_kernel,
        grid=(batch, MLA_HEADS // ATTN_HEADS, nq),
        in_specs=[
            pl.BlockSpec((None, blk, ATTN_HEADS * MLA_QK_PAD), lambda b, h, i: (b, i, h)),
            pl.BlockSpec((None, blk, ATTN_HEADS * MLA_QK_PAD),
                         lambda b, h, i: (b, jnp.minimum(i + 1, nq - 1), h)),
            pl.BlockSpec((None, seq, ATTN_HEADS * MLA_QK_PAD), lambda b, h, i: (b, 0, h)),
            pl.BlockSpec((None, nq, ATTN_HEADS * MLA_DV, blk), lambda b, h, i: (b, 0, h, 0)),
        ],
        out_specs=pl.BlockSpec((None, blk, ATTN_HEADS * MLA_DV), lambda b, h, i: (b, i, h)),
        out_shape=jax.ShapeDtypeStruct((batch, seq, MLA_WIDTH), BF16),
        scratch_shapes=[pltpu.VMEM((2, ATTN_HEADS, blk, blk), F32),
                        pltpu.VMEM((ATTN_HEADS, MLA_DV, blk), F32),
                        pltpu.VMEM((ATTN_HEADS, 1, blk), F32), pltpu.VMEM((ATTN_HEADS, 1, blk), F32)],
        compiler_params=pltpu.CompilerParams(
            dimension_semantics=("arbitrary", "arbitrary", "arbitrary"),
            vmem_limit_bytes=VMEM_LIMIT),
        name="mla_attention",
    )(qm, qm, km, vt)
    return out.reshape(batch * seq, MLA_WIDTH)


def _outproj_kernel(x_ref, mod_ref, yr_ref, ym_ref, w_ref, o_ref):
    y = jnp.dot(yr_ref[...], w_ref[:RET_WIDTH, :], preferred_element_type=F32)
    y += jnp.dot(ym_ref[...], w_ref[RET_WIDTH:, :], preferred_element_type=F32)
    o_ref[...] = x_ref[...] + mod_ref[5:6, :] * y


def _outproj(x, mod, y_r, y_m, w_out, *, layer, seq):
    t, d = x.shape
    tm = 512
    row = lambda i: (i, 0)
    return pl.pallas_call(
        _outproj_kernel,
        grid=(t // tm,),
        in_specs=[
            pl.BlockSpec((tm, d), row),
            pl.BlockSpec((None, None, N_MOD, d), lambda i: (layer, i // (seq // tm), 0, 0)),
            pl.BlockSpec((tm, RET_WIDTH), row),
            pl.BlockSpec((tm, MLA_WIDTH), row),
            pl.BlockSpec((None,) + w_out.shape[1:], lambda i: (layer, 0, 0),
                         pipeline_mode=pl.Buffered(1)),
        ],
        out_specs=pl.BlockSpec((tm, d), row),
        out_shape=jax.ShapeDtypeStruct((t, d), F32),
        compiler_params=pltpu.CompilerParams(
            dimension_semantics=("arbitrary",), vmem_limit_bytes=VMEM_LIMIT),
        name="mixer_outproj",
    )(x, mod, y_r, y_m, w_out)


def kernel(x, c, positions, w_ada, b_ada, norm_ffn1, ffn1_w_gu, ffn1_w_down, norm_mix, w_in,
           ret_norm_g, q_norm_g, w_uq, kv_norm_g, w_ukv, w_out, norm_ffn2, ffn2_w_gu,
           ffn2_w_down, final_norm):
    batch, seq, d = x.shape
    depth = w_ada.shape[0]
    xt = x.reshape(batch * seq, d)

    c_pad = jnp.pad(c, ((0, 8 - batch), (0, 0)))
    mod = _adaln_mod(c_pad, w_ada, b_ada)[:, :batch].reshape(depth, batch, N_MOD, d)
    cos, sin = _rope_tables(positions)

    w_in_b = jnp.pad(w_in, ((0, 0), (0, 0), (0, IN_COLS_PAD - w_in.shape[-1]))).astype(BF16)
    w_uq_b = jnp.pad(w_uq.reshape(depth, MLA_Q_RANK, MLA_HEADS, MLA_NOPE + MLA_ROPE),
                     ((0, 0), (0, 0), (0, 0), (0, MLA_QK_PAD - MLA_NOPE - MLA_ROPE))
                     ).reshape(depth, MLA_Q_RANK, MLA_HEADS * MLA_QK_PAD).astype(BF16)
    w_ukv_b = _to_bf16(w_ukv)
    w_out_b = _to_bf16(w_out)
    ffn_f32 = [(ffn1_w_gu, ffn1_w_down), (ffn2_w_gu, ffn2_w_down)]
    ffn_gain = [norm_ffn1, norm_ffn2]
    ffn_w = (_to_bf16(ffn1_w_gu, layer=0), _to_bf16(ffn1_w_down, layer=0))

    def ffn(xt, weights, l, which):
        last = l == depth - 1 and which == 1
        nxt_l, nxt_which = (l, 1) if which == 0 else (l + 1, 0)
        side = None if last else ffn_f32[nxt_which] + (nxt_l,)
        return _ffn(xt, mod, ffn_gain[which][l], weights[0], weights[1], final_norm, side,
                    layer=l, mod_base=6 * which, final_norm=last, seq=seq)

    for l in range(depth):
        xt, ffn_w = ffn(xt, ffn_w, l, 0)
        qk, v_r, g_r, qm, km, vt = _inproj(xt, mod, norm_mix[l], cos, sin, w_in_b, q_norm_g[l],
                                           w_uq_b, kv_norm_g[l], w_ukv_b, layer=l, seq=seq)
        y_r = _retention(qk, v_r, g_r, ret_norm_g[l], batch=batch, seq=seq)
        y_m = _attention(qm, km, vt, batch=batch, seq=seq)
        xt = _outproj(xt, mod, y_r, y_m, w_out_b, layer=l, seq=seq)
        xt, ffn_w = ffn(xt, ffn_w, l, 1)
    return xt.reshape(batch, seq, d)
```

```python
import functools

import jax
import jax.numpy as jnp
import numpy as np
from jax import lax
from jax.experimental import pallas as pl
from jax.experimental.pallas import tpu as pltpu

D_MODEL = 2048
DEPTH = 2
RET_HEADS = 8
RET_DK = 64
RET_DV = 128
RET_CHUNK = 128
MLA_HEADS = 8
MLA_Q_RANK = 512
MLA_KV_RANK = 256
MLA_NOPE = 128
MLA_ROPE = 64
MLA_DV = 128
RET_WIDTH = RET_HEADS * RET_DV
MLA_WIDTH = MLA_HEADS * MLA_DV
ROPE_DIM = 64
ROPE_BASE = 10000.0
EPS = 1e-6
N_MOD = 9

LANES = 128
MXU_COLS = 256
MLA_QK_PAD = 256
IN_QK = 2 * RET_HEADS * RET_DK
IN_V0 = IN_QK
IN_G0 = IN_V0 + RET_WIDTH
IN_CQ0 = IN_G0 + RET_WIDTH
IN_CKV0 = IN_CQ0 + MLA_Q_RANK
IN_KPE0 = IN_CKV0 + MLA_KV_RANK
IN_COLS_PAD = IN_KPE0 + LANES

VMEM_LIMIT = 56 * 1024 * 1024
ATTN_BLK = 512
ATTN_HEADS = 4
LOG2_E = 1.4426950408889634
CAST_BLOCK_BYTES = 8 * 1024 * 1024
FFN_UP_TM = 1024
FFN_DOWN_TM = 512
FFN_DOWN_TN = 1024
FFN_TF = 512
BF16_SUBLANES = 16

BF16 = jnp.bfloat16
F32 = jnp.float32


def _rms(xf, gain):
    return xf * lax.rsqrt(jnp.mean(xf * xf, axis=-1, keepdims=True) + EPS) * gain


def _rope(blk, cos, sin_signed):
    lane = lax.broadcasted_iota(jnp.int32, blk.shape, 1)
    partner = jnp.where(lane % ROPE_DIM < ROPE_DIM // 2,
                        pltpu.roll(blk, LANES - ROPE_DIM // 2, 1),
                        pltpu.roll(blk, ROPE_DIM // 2, 1))
    return blk * cos + partner * sin_signed


def _cast_kernel(w_ref, o_ref):
    cols = w_ref.shape[-1]
    o_ref[:, :cols] = w_ref[...].astype(BF16)
    if o_ref.shape[-1] > cols:
        o_ref[:, cols:] = jnp.zeros((o_ref.shape[0], o_ref.shape[-1] - cols), BF16)


def _to_bf16(w, out_cols=None, layer=None):
    depth, rows, cols = w.shape
    out_cols = out_cols or cols
    tr = 16
    while rows % (2 * tr) == 0 and 2 * tr * cols * 4 <= CAST_BLOCK_BYTES:
        tr *= 2
    if layer is None:
        grid = (depth, rows // tr)
        in_spec = pl.BlockSpec((None, tr, cols), lambda l, i: (l, i, 0))
        out_spec = pl.BlockSpec((None, tr, out_cols), lambda l, i: (l, i, 0))
        out_shape = (depth, rows, out_cols)
    else:
        grid = (rows // tr,)
        in_spec = pl.BlockSpec((None, tr, cols), lambda i: (layer, i, 0))
        out_spec = pl.BlockSpec((tr, out_cols), lambda i: (i, 0))
        out_shape = (rows, out_cols)
    return pl.pallas_call(
        _cast_kernel,
        grid=grid,
        in_specs=[in_spec],
        out_specs=out_spec,
        out_shape=jax.ShapeDtypeStruct(out_shape, BF16),
        compiler_params=pltpu.CompilerParams(
            dimension_semantics=("arbitrary",) * len(grid), vmem_limit_bytes=VMEM_LIMIT),
        name="cast_bf16",
    )(w)


def _mod_kernel(c_ref, w_ref, b_ref, o_ref):
    ca = jax.nn.silu(c_ref[...]).astype(BF16)
    o_ref[...] = jnp.dot(ca, w_ref[...].astype(BF16), preferred_element_type=F32) + b_ref[...]


def _adaln_mod(c_pad, w_ada, b_ada):
    depth, d, n = w_ada.shape
    rows = c_pad.shape[0]
    tn = 1024
    return pl.pallas_call(
        _mod_kernel,
        grid=(depth, n // tn),
        in_specs=[
            pl.BlockSpec((rows, d), lambda l, j: (0, 0)),
            pl.BlockSpec((None, d, tn), lambda l, j: (l, 0, j)),
            pl.BlockSpec((None, 1, tn), lambda l, j: (l, 0, j)),
        ],
        out_specs=pl.BlockSpec((None, rows, tn), lambda l, j: (l, 0, j)),
        out_shape=jax.ShapeDtypeStruct((depth, rows, n), F32),
        compiler_params=pltpu.CompilerParams(
            dimension_semantics=("arbitrary", "arbitrary"), vmem_limit_bytes=VMEM_LIMIT),
        name="adaln_mod",
    )(c_pad, w_ada, b_ada.reshape(depth, 1, n))


def _rope_table_kernel(pos_ref, inv_ref, sign_ref, cos_ref, sin_ref):
    ang = pos_ref[...].astype(F32) * inv_ref[...]
    cos_ref[...] = jnp.cos(ang)
    sin_ref[...] = jnp.sin(ang) * sign_ref[...]


def _rope_tables(positions):
    t = positions.size
    tm = 2048
    inv = ROPE_BASE ** (-jnp.arange(0, ROPE_DIM, 2, dtype=F32) / ROPE_DIM)
    inv_l = jnp.tile(inv, LANES // (ROPE_DIM // 2)).reshape(1, LANES)
    sign = np.where(np.arange(LANES) % ROPE_DIM < ROPE_DIM // 2, -1.0, 1.0).astype(np.float32)
    return pl.pallas_call(
        _rope_table_kernel,
        grid=(t // tm,),
        in_specs=[
            pl.BlockSpec((tm, 1), lambda i: (i, 0)),
            pl.BlockSpec((1, LANES), lambda i: (0, 0)),
            pl.BlockSpec((1, LANES), lambda i: (0, 0)),
        ],
        out_specs=[pl.BlockSpec((tm, LANES), lambda i: (i, 0))] * 2,
        out_shape=[jax.ShapeDtypeStruct((t, LANES), F32)] * 2,
        compiler_params=pltpu.CompilerParams(dimension_semantics=("arbitrary",)),
        name="rope_tables",
    )(positions.reshape(t, 1), inv_l, jnp.asarray(sign).reshape(1, LANES))


def _ffn_up_kernel(*refs, mod_base, nchunks, side_cast):
    x0_ref, xnext_ref, mod_ref, modnext_ref, gain_ref, wg_ref, wu_ref = refs[:7]
    if side_cast:
        side_gu_ref, side_d_ref, a_ref, side_gu_out, side_d_out, h_even, h_odd = refs[7:]
    else:
        a_ref, h_even, h_odd = refs[7:]
    i = pl.program_id(0)
    j = pl.program_id(1)
    chunk = xnext_ref.shape[0]

    def norm_mod(xf, m_ref):
        y = _rms(xf, gain_ref[...])
        return (y * (1 + m_ref[mod_base + 1:mod_base + 2, :]) + m_ref[mod_base:mod_base + 1, :]
                ).astype(BF16)

    @pl.when((i == 0) & (j == 0))
    def _():
        h_even[...] = norm_mod(x0_ref[...], mod_ref)

    def step(h_ref, hnext_ref):
        zero_bits = None
        for c0 in range(0, wg_ref.shape[1], MXU_COLS):
            cols = slice(c0, c0 + MXU_COLS)
            wg = wg_ref[:, cols]
            if zero_bits is not None:
                wbits = pltpu.bitcast(wg, jnp.uint32)
                wbits = (wbits.reshape(-1, 8, MXU_COLS) | zero_bits[None]).reshape(wbits.shape)
                wg = pltpu.bitcast(wbits, BF16)
            g = jnp.dot(h_ref[...], wg, preferred_element_type=F32)
            u = jnp.dot(h_ref[...], wu_ref[:, cols], preferred_element_type=F32)
            a_ref[:, cols] = (jax.nn.silu(g) * u).astype(BF16)
            if c0 == 0:
                rows = pl.ds(pl.multiple_of(jnp.minimum(j, nchunks - 1) * chunk, chunk), chunk)
                hn = norm_mod(xnext_ref[...], modnext_ref)
                hnext_ref[rows, :] = hn
                bits = pltpu.bitcast(hn, jnp.uint32)
                bits = (bits >> 16) >> 16
                bits = functools.reduce(jnp.bitwise_or, [bits[r:r + 8] for r in range(0, bits.shape[0], 8)])
                zero_bits = functools.reduce(
                    jnp.bitwise_or, [bits[:, c:c + MXU_COLS] for c in range(0, bits.shape[1], MXU_COLS)])
                if side_cast:
                    side_gu_out[...] = side_gu_ref[...].astype(BF16)
                    side_d_out[...] = side_d_ref[...].astype(BF16)

    @pl.when(i % 2 == 0)
    def _():
        step(h_even, h_odd)

    @pl.when(i % 2 == 1)
    def _():
        step(h_odd, h_even)


def _ffn_down_kernel(a_ref, wd_ref, x_ref, mod_ref, fgain_ref, o_ref, *, mod_base, final_norm):
    y = jnp.dot(a_ref[...], wd_ref[...], preferred_element_type=F32)
    out = x_ref[...] + 0.5 * mod_ref[mod_base + 2:mod_base + 3, :] * y
    if final_norm:
        out = _rms(out, fgain_ref[...])
    o_ref[...] = out


def _ffn(x, mod, gain, w_gu, w_down, final_gain, side, *, layer, mod_base, final_norm, seq):
    t, d = x.shape
    f = w_down.shape[0]
    tm, tf = FFN_UP_TM, FFN_TF
    nf = f // tf
    nt = t // tm
    nchunks = 8
    assert nchunks <= nf and tm % nchunks == 0
    chunk = tm // nchunks
    nxt = lambda i: jnp.minimum(i + 1, nt - 1)
    in_specs = [
        pl.BlockSpec((tm, d), lambda i, j: (0, 0), pipeline_mode=pl.Buffered(1)),
        pl.BlockSpec((chunk, d), lambda i, j: (nxt(i) * nchunks + jnp.minimum(j, nchunks - 1), 0)),
        pl.BlockSpec((None, None, N_MOD, d), lambda i, j: (layer, i // (seq // tm), 0, 0)),
        pl.BlockSpec((None, None, N_MOD, d), lambda i, j: (layer, nxt(i) // (seq // tm), 0, 0)),
        pl.BlockSpec((1, d), lambda i, j: (0, 0)),
        pl.BlockSpec((d, tf), lambda i, j: (0, j)),
        pl.BlockSpec((d, tf), lambda i, j: (0, nf + j)),
    ]
    operands = [x, x, mod, mod, gain.reshape(1, d), w_gu, w_gu]
    out_specs = [pl.BlockSpec((tm, tf), lambda i, j: (i, j))]
    out_shape = [jax.ShapeDtypeStruct((t, f), BF16)]
    if side is not None:
        side_layer = side[2]
        for w in side[:2]:
            _, rows, cols = w.shape
            rps = -(-rows // (BF16_SUBLANES * nt * nf)) * BF16_SUBLANES
            assert rows % rps == 0
            blk = lambda i, j, last=rows // rps - 1: jnp.minimum(i * nf + j, last)
            in_specs.append(pl.BlockSpec((None, rps, cols),
                                         lambda i, j, blk=blk: (side_layer, blk(i, j), 0)))
            operands.append(w)
            out_specs.append(pl.BlockSpec((rps, cols), lambda i, j, blk=blk: (blk(i, j), 0)))
            out_shape.append(jax.ShapeDtypeStruct((rows, cols), BF16))
    outs = pl.pallas_call(
        functools.partial(_ffn_up_kernel, mod_base=mod_base, nchunks=nchunks,
                          side_cast=side is not None),
        grid=(nt, nf),
        in_specs=in_specs,
        out_specs=out_specs,
        out_shape=out_shape,
        scratch_shapes=[pltpu.VMEM((tm, d), BF16), pltpu.VMEM((tm, d), BF16)],
        compiler_params=pltpu.CompilerParams(
            dimension_semantics=("arbitrary", "arbitrary"), vmem_limit_bytes=VMEM_LIMIT),
        name="ffn_gate_up",
    )(*operands)

    tm2, tn = (FFN_DOWN_TM // 2, d) if final_norm else (FFN_DOWN_TM, FFN_DOWN_TN)
    wd_mode = dict(pipeline_mode=pl.Buffered(1)) if final_norm else {}
    y = pl.pallas_call(
        functools.partial(_ffn_down_kernel, mod_base=mod_base, final_norm=final_norm),
        grid=(d // tn, t // tm2),
        in_specs=[
            pl.BlockSpec((tm2, f), lambda n, i: (i, 0)),
            pl.BlockSpec((f, tn), lambda n, i: (0, n), **wd_mode),
            pl.BlockSpec((tm2, tn), lambda n, i: (i, n)),
            pl.BlockSpec((None, None, N_MOD, tn), lambda n, i: (layer, i // (seq // tm2), 0, n)),
            pl.BlockSpec((1, tn), lambda n, i: (0, n)),
        ],
        out_specs=pl.BlockSpec((tm2, tn), lambda n, i: (i, n)),
        out_shape=jax.ShapeDtypeStruct((t, d), F32),
        compiler_params=pltpu.CompilerParams(
            dimension_semantics=("arbitrary", "arbitrary"), vmem_limit_bytes=VMEM_LIMIT),
        name="ffn_down",
    )(outs[0], w_down, x, mod, final_gain.reshape(1, d))
    return y, tuple(outs[1:])


def _inproj_kernel(x_ref, mod_ref, gain_ref, cos_ref, sin_ref, win_ref, qg_ref, wuq_ref,
                   kvg_ref, wukv_ref, qk_ref, v_ref, g_ref, qm_ref, km_ref, vm_ref):
    y = _rms(x_ref[...], gain_ref[...])
    h = (y * (1 + mod_ref[4:5, :]) + mod_ref[3:4, :]).astype(BF16)
    cos = cos_ref[...]
    sin = sin_ref[...]

    def proj(c0, c1):
        return jnp.dot(h, win_ref[:, c0:c1], preferred_element_type=F32)

    half = IN_QK // 2
    for c0, mult in ((0, 1.0), (half, RET_DK ** -0.5)):
        p = proj(c0, c0 + half)
        for t in range(half // LANES):
            r = _rope(p[:, t * LANES:(t + 1) * LANES], cos, sin)
            qk_ref[:, c0 + t * LANES:c0 + (t + 1) * LANES] = (r * mult).astype(BF16)
    v_ref[...] = proj(IN_V0, IN_G0).astype(BF16)
    g_ref[...] = proj(IN_G0, IN_CQ0).astype(BF16)

    cq = _rms(proj(IN_CQ0, IN_CKV0), qg_ref[...]).astype(BF16)
    q = jnp.dot(cq, wuq_ref[...], preferred_element_type=F32)
    qscale = (MLA_NOPE + MLA_ROPE) ** -0.5 * LOG2_E
    for hd in range(MLA_HEADS):
        b0 = hd * MLA_QK_PAD
        qm_ref[:, b0:b0 + LANES] = (q[:, b0:b0 + LANES] * qscale).astype(BF16)
        r = _rope(q[:, b0 + LANES:b0 + 2 * LANES], cos, sin)
        qm_ref[:, b0 + LANES:b0 + 2 * LANES] = (r * qscale).astype(BF16)

    tail = proj(IN_CKV0, IN_COLS_PAD)
    ckv = _rms(tail[:, :MLA_KV_RANK], kvg_ref[...]).astype(BF16)
    kpe = _rope(tail[:, MLA_KV_RANK:], cos, sin).astype(BF16)
    kv = jnp.dot(ckv, wukv_ref[...], preferred_element_type=F32)
    for hd in range(MLA_HEADS):
        b0 = hd * (MLA_NOPE + MLA_DV)
        km_ref[:, hd * MLA_QK_PAD:hd * MLA_QK_PAD + LANES] = kv[:, b0:b0 + MLA_NOPE].astype(BF16)
        km_ref[:, hd * MLA_QK_PAD + LANES:(hd + 1) * MLA_QK_PAD] = kpe
        vm_ref[hd * MLA_DV:(hd + 1) * MLA_DV, :] = (
            kv[:, b0 + MLA_NOPE:b0 + MLA_NOPE + MLA_DV].T.astype(BF16))


def _inproj(x, mod, gain, cos, sin, w_in, q_gain, w_uq, kv_gain, w_ukv, *, layer, seq):
    t, d = x.shape
    tm = 256
    const = lambda i: (0, 0)
    row = lambda i: (i, 0)

    def resident(w):
        return pl.BlockSpec((None,) + w.shape[1:], lambda i: (layer, 0, 0),
                            pipeline_mode=pl.Buffered(1))

    widths = [IN_QK, RET_WIDTH, RET_WIDTH, MLA_HEADS * MLA_QK_PAD, MLA_HEADS * MLA_QK_PAD]
    tpb = seq // tm
    sub = ATTN_BLK // tm
    vt_spec = pl.BlockSpec((None, None, MLA_WIDTH, tm),
                           lambda i: (i // tpb, (i % tpb) // sub, 0, i % sub))
    vt_shape = jax.ShapeDtypeStruct((t // seq, seq // ATTN_BLK, MLA_WIDTH, ATTN_BLK), BF16)
    return pl.pallas_call(
        _inproj_kernel,
        grid=(t // tm,),
        in_specs=[
            pl.BlockSpec((tm, d), row),
            pl.BlockSpec((None, None, N_MOD, d), lambda i: (layer, i // (seq // tm), 0, 0)),
            pl.BlockSpec((1, d), const),
            pl.BlockSpec((tm, LANES), row),
            pl.BlockSpec((tm, LANES), row),
            resident(w_in),
            pl.BlockSpec((1, MLA_Q_RANK), const),
            resident(w_uq),
            pl.BlockSpec((1, MLA_KV_RANK), const),
            resident(w_ukv),
        ],
        out_specs=[pl.BlockSpec((tm, w), row) for w in widths] + [vt_spec],
        out_shape=[jax.ShapeDtypeStruct((t, w), BF16) for w in widths] + [vt_shape],
        compiler_params=pltpu.CompilerParams(
            dimension_semantics=("arbitrary",), vmem_limit_bytes=VMEM_LIMIT),
        name="mixer_inproj",
    )(x, mod, gain.reshape(1, d), cos, sin, w_in, q_gain.reshape(1, -1), w_uq,
      kv_gain.reshape(1, -1), w_ukv)


def _retention_kernel(qk_ref, v_ref, g_ref, intra_ref, qdec_ref, kdec_ref, cdec_ref, hmask_ref,
                      gain_ref, o_ref, state_sc, *, chunks):
    @pl.when(pl.program_id(1) == 0)
    def _():
        state_sc[...] = jnp.zeros_like(state_sc)

    kbase = RET_HEADS * RET_DK

    def chunk_body(c, carry):
        rows = pl.ds(pl.multiple_of(c * RET_CHUNK, RET_CHUNK), RET_CHUNK)
        for pair in range(RET_HEADS // 2):
            qf = qk_ref[rows, pair * LANES:(pair + 1) * LANES].astype(F32)
            kp = qk_ref[rows, kbase + pair * LANES:kbase + (pair + 1) * LANES]
            kdt = (kp.astype(F32) * kdec_ref[pair]).T.astype(BF16)
            for hh in range(2):
                hd = 2 * pair + hh
                cols = slice(hd * RET_DV, (hd + 1) * RET_DV)
                vh = v_ref[rows, cols]
                qm = (qf * hmask_ref[hh]).astype(BF16)
                s = lax.dot_general(qm, kp, (((1,), (1,)), ((), ())), preferred_element_type=F32)
                intra = jnp.dot((s * intra_ref[hd]).astype(BF16), vh, preferred_element_type=F32)
                qd = (qf * qdec_ref[hd]).astype(BF16)
                state = state_sc[hd]
                cross = jnp.dot(qd, state.astype(BF16), preferred_element_type=F32)
                state_sc[hd] = state * cdec_ref[hd] + jnp.dot(kdt, vh, preferred_element_type=F32)
                yh = _rms(intra + cross, gain_ref[:, cols])
                o_ref[rows, cols] = (jax.nn.silu(g_ref[rows, cols].astype(F32)) * yh).astype(BF16)
        return carry

    lax.fori_loop(0, chunks, chunk_body, 0)


def _retention_consts():
    c = RET_CHUNK
    log_g = jnp.log1p(-(2.0 ** (-5.0 - jnp.arange(RET_HEADS, dtype=F32))))
    idx = jnp.arange(c, dtype=F32)
    rel = idx[:, None] - idx[None, :]
    intra = jnp.where(rel >= 0, jnp.exp(log_g[:, None, None] * jnp.maximum(rel, 0.0)), 0.0)
    q_decay = jnp.exp(log_g[:, None] * (idx + 1.0))
    k_decay = jnp.exp(log_g[:, None] * (c - 1.0 - idx))
    chunk_decay = jnp.exp(log_g * c)
    own = (np.arange(LANES)[None, :] // RET_DK == np.arange(2)[:, None]).astype(np.float32)
    hmask = jnp.asarray(own).reshape(2, 1, LANES)
    qdec = q_decay[:, :, None] * jnp.asarray(own)[jnp.arange(RET_HEADS) % 2][:, None, :]
    kdec = jnp.repeat(k_decay.reshape(RET_HEADS // 2, 2, c), RET_DK, axis=1).transpose(0, 2, 1)
    cdec = jnp.broadcast_to(chunk_decay[:, None, None], (RET_HEADS, 1, LANES))
    return intra, qdec, kdec, cdec, hmask


def _retention(qk, v, g, gain, *, batch, seq):
    tb = 1024
    chunks = tb // RET_CHUNK
    nb = seq // tb
    intra, qdec, kdec, cdec, hmask = _retention_consts()
    row = lambda b, i: (b * nb + i, 0)
    c3 = lambda b, i: (0, 0, 0)
    return pl.pallas_call(
        functools.partial(_retention_kernel, chunks=chunks),
        grid=(batch, nb),
        in_specs=[
            pl.BlockSpec((tb, IN_QK), row),
            pl.BlockSpec((tb, RET_WIDTH), row),
            pl.BlockSpec((tb, RET_WIDTH), row),
            pl.BlockSpec(intra.shape, c3),
            pl.BlockSpec(qdec.shape, c3),
            pl.BlockSpec(kdec.shape, c3),
            pl.BlockSpec(cdec.shape, c3),
            pl.BlockSpec(hmask.shape, c3),
            pl.BlockSpec((1, RET_WIDTH), lambda b, i: (0, 0)),
        ],
        out_specs=pl.BlockSpec((tb, RET_WIDTH), row),
        out_shape=jax.ShapeDtypeStruct((batch * seq, RET_WIDTH), BF16),
        scratch_shapes=[pltpu.VMEM((RET_HEADS, LANES, RET_DV), F32)],
        compiler_params=pltpu.CompilerParams(
            dimension_semantics=("arbitrary", "arbitrary"), vmem_limit_bytes=VMEM_LIMIT),
        name="retention",
    )(qk, v, g, intra, qdec, kdec, cdec, hmask, gain.reshape(1, RET_WIDTH))


def _attn_kernel(q_ref, qnext_ref, k_ref, vt_ref, o_ref, s_sc, acc_sc, m_sc, l_sc):
    qi = pl.program_id(2)
    blk = ATTN_BLK
    acc_sc[...] = jnp.zeros_like(acc_sc)

    def scores(ki, slot, queries=q_ref):
        rows = pl.ds(pl.multiple_of(ki * blk, blk), blk)
        for hd in range(ATTN_HEADS):
            cols = slice(hd * MLA_QK_PAD, (hd + 1) * MLA_QK_PAD)
            s_sc[slot, hd] = lax.dot_general(k_ref[rows, cols], queries[:, cols],
                                             (((1,), (1,)), ((), ())), preferred_element_type=F32)

    def update(ki, slot, stats, masked):
        new_stats = []
        for hd in range(ATTN_HEADS):
            m_prev, l_prev = stats[hd]
            s = s_sc[slot, hd]
            if masked:
                kpos = lax.broadcasted_iota(jnp.int32, s.shape, 0)
                qpos = lax.broadcasted_iota(jnp.int32, s.shape, 1)
                s = jnp.where(kpos <= qpos, s, -jnp.inf)
            m_new = jnp.maximum(m_prev, jnp.max(s, axis=0, keepdims=True))
            p = jnp.exp2(s - m_new)
            alpha = jnp.exp2(m_prev - m_new)
            l_new = alpha * l_prev + jnp.sum(p, axis=0, keepdims=True)
            vt = vt_ref[ki, hd * MLA_DV:(hd + 1) * MLA_DV, :]
            acc_sc[hd] = alpha * acc_sc[hd] + jnp.dot(vt, p.astype(BF16), preferred_element_type=F32)
            new_stats.append((m_new, l_new))
        return tuple(new_stats)

    def load_stats():
        return tuple((m_sc[hd], l_sc[hd]) for hd in range(ATTN_HEADS))

    def store_stats(stats):
        for hd in range(ATTN_HEADS):
            m_sc[hd], l_sc[hd] = stats[hd]

    @pl.when(qi == 0)
    def _():
        scores(0, 0)

    def block_pair(kp, stats):
        scores(2 * kp + 1, 1)
        stats = update(2 * kp, 0, stats, False)
        scores(2 * kp + 2, 0)
        stats = update(2 * kp + 1, 1, stats, False)
        return stats

    init = tuple((jnp.full((1, blk), -jnp.inf, F32), jnp.zeros((1, blk), F32))
                 for _ in range(ATTN_HEADS))
    store_stats(lax.fori_loop(0, qi // 2, block_pair, init))

    @pl.when(qi % 2 == 1)
    def _():
        stats = update(qi - 1, 0, load_stats(), False)
        scores(qi, 1)
        store_stats(stats)

    stats = update(qi, qi % 2, load_stats(), True)
    scores(0, 0, qnext_ref)
    for hd in range(ATTN_HEADS):
        out_t = acc_sc[hd] / stats[hd][1]
        o_ref[:, hd * MLA_DV:(hd + 1) * MLA_DV] = out_t.T.astype(BF16)


def _attention(qm, km, vt, *, batch, seq):
    blk = ATTN_BLK
    nq = seq // blk
    qm = qm.reshape(batch, seq, -1)
    km = km.reshape(batch, seq, -1)
    out = pl.pallas_call(
        _attn_kernel,
        grid=(batch, MLA_HEADS // ATTN_HEADS, nq),
        in_specs=[
            pl.BlockSpec((None, blk, ATTN_HEADS * MLA_QK_PAD), lambda b, h, i: (b, i, h)),
            pl.BlockSpec((None, blk, ATTN_HEADS * MLA_QK_PAD),
                         lambda b, h, i: (b, jnp.minimum(i + 1, nq - 1), h)),
            pl.BlockSpec((None, seq, ATTN_HEADS * MLA_QK_PAD), lambda b, h, i: (b, 0, h)),
            pl.BlockSpec((None, nq, ATTN_HEADS * MLA_DV, blk), lambda b, h, i: (b, 0, h, 0)),
        ],
        out_specs=pl.BlockSpec((None, blk, ATTN_HEADS * MLA_DV), lambda b, h, i: (b, i, h)),
        out_shape=jax.ShapeDtypeStruct((batch, seq, MLA_WIDTH), BF16),
        scratch_shapes=[pltpu.VMEM((2, ATTN_HEADS, blk, blk), F32),
                        pltpu.VMEM((ATTN_HEADS, MLA_DV, blk), F32),
                        pltpu.VMEM((ATTN_HEADS, 1, blk), F32), pltpu.VMEM((ATTN_HEADS, 1, blk), F32)],
        compiler_params=pltpu.CompilerParams(
            dimension_semantics=("arbitrary", "arbitrary", "arbitrary"),
            vmem_limit_bytes=VMEM_LIMIT),
        name="mla_attention",
    )(qm, qm, km, vt)
    return out.reshape(batch * seq, MLA_WIDTH)


def _outproj_kernel(x_ref, mod_ref, yr_ref, ym_ref, w_ref, o_ref):
    y = jnp.dot(yr_ref[...], w_ref[:RET_WIDTH, :], preferred_element_type=F32)
    y += jnp.dot(ym_ref[...], w_ref[RET_WIDTH:, :], preferred_element_type=F32)
    o_ref[...] = x_ref[...] + mod_ref[5:6, :] * y


def _outproj(x, mod, y_r, y_m, w_out, *, layer, seq):
    t, d = x.shape
    tm = 512
    row = lambda i: (i, 0)
    return pl.pallas_call(
        _outproj_kernel,
        grid=(t // tm,),
        in_specs=[
            pl.BlockSpec((tm, d), row),
            pl.BlockSpec((None, None, N_MOD, d), lambda i: (layer, i // (seq // tm), 0, 0)),
            pl.BlockSpec((tm, RET_WIDTH), row),
            pl.BlockSpec((tm, MLA_WIDTH), row),
            pl.BlockSpec((None,) + w_out.shape[1:], lambda i: (layer, 0, 0),
                         pipeline_mode=pl.Buffered(1)),
        ],
        out_specs=pl.BlockSpec((tm, d), row),
        out_shape=jax.ShapeDtypeStruct((t, d), F32),
        compiler_params=pltpu.CompilerParams(
            dimension_semantics=("arbitrary",), vmem_limit_bytes=VMEM_LIMIT),
        name="mixer_outproj",
    )(x, mod, y_r, y_m, w_out)


def kernel(x, c, positions, w_ada, b_ada, norm_ffn1, ffn1_w_gu, ffn1_w_down, norm_mix, w_in,
           ret_norm_g, q_norm_g, w_uq, kv_norm_g, w_ukv, w_out, norm_ffn2, ffn2_w_gu,
           ffn2_w_down, final_norm):
    batch, seq, d = x.shape
    depth = w_ada.shape[0]
    xt = x.reshape(batch * seq, d)

    c_pad = jnp.pad(c, ((0, 8 - batch), (0, 0)))
    mod = _adaln_mod(c_pad, w_ada, b_ada)[:, :batch].reshape(depth, batch, N_MOD, d)
    cos, sin = _rope_tables(positions)

    w_in_b = jnp.pad(w_in, ((0, 0), (0, 0), (0, IN_COLS_PAD - w_in.shape[-1]))).astype(BF16)
    w_uq_b = jnp.pad(w_uq.reshape(depth, MLA_Q_RANK, MLA_HEADS, MLA_NOPE + MLA_ROPE),
                     ((0, 0), (0, 0), (0, 0), (0, MLA_QK_PAD - MLA_NOPE - MLA_ROPE))
                     ).reshape(depth, MLA_Q_RANK, MLA_HEADS * MLA_QK_PAD).astype(BF16)
    w_ukv_b = _to_bf16(w_ukv)
    w_out_b = _to_bf16(w_out)
    ffn_f32 = [(ffn1_w_gu, ffn1_w_down), (ffn2_w_gu, ffn2_w_down)]
    ffn_gain = [norm_ffn1, norm_ffn2]
    ffn_w = (_to_bf16(ffn1_w_gu, layer=0), _to_bf16(ffn1_w_down, layer=0))

    def ffn(xt, weights, l, which):
        last = l == depth - 1 and which == 1
        nxt_l, nxt_which = (l, 1) if which == 0 else (l + 1, 0)
        side = None if last else ffn_f32[nxt_which] + (nxt_l,)
        return _ffn(xt, mod, ffn_gain[which][l], weights[0], weights[1], final_norm, side,
                    layer=l, mod_base=6 * which, final_norm=last, seq=seq)

    for l in range(depth):
        xt, ffn_w = ffn(xt, ffn_w, l, 0)
        qk, v_r, g_r, qm, km, vt = _inproj(xt, mod, norm_mix[l], cos, sin, w_in_b, q_norm_g[l],
                                           w_uq_b, kv_norm_g[l], w_ukv_b, layer=l, seq=seq)
        y_r = _retention(qk, v_r, g_r, ret_norm_g[l], batch=batch, seq=seq)
        y_m = _attention(qm, km, vt, batch=batch, seq=seq)
        xt = _outproj(xt, mod, y_r, y_m, w_out_b, layer=l, seq=seq)
        xt, ffn_w = ffn(xt, ffn_w, l, 1)
    return xt.reshape(batch, seq, d)
```

```python
import functools

import jax
import jax.numpy as jnp
import numpy as np
from jax import lax
from jax.experimental import pallas as pl
from jax.experimental.pallas import tpu as pltpu

D_MODEL = 2048
DEPTH = 2
RET_HEADS = 8
RET_DK = 64
RET_DV = 128
RET_CHUNK = 128
MLA_HEADS = 8
MLA_Q_RANK = 512
MLA_KV_RANK = 256
MLA_NOPE = 128
MLA_ROPE = 64
MLA_DV = 128
RET_WIDTH = RET_HEADS * RET_DV
MLA_WIDTH = MLA_HEADS * MLA_DV
ROPE_DIM = 64
ROPE_BASE = 10000.0
EPS = 1e-6
N_MOD = 9

LANES = 128
MXU_COLS = 256
MLA_QK_PAD = 256
IN_QK = 2 * RET_HEADS * RET_DK
IN_V0 = IN_QK
IN_G0 = IN_V0 + RET_WIDTH
IN_CQ0 = IN_G0 + RET_WIDTH
IN_CKV0 = IN_CQ0 + MLA_Q_RANK
IN_KPE0 = IN_CKV0 + MLA_KV_RANK
IN_COLS_PAD = IN_KPE0 + LANES

VMEM_LIMIT = 56 * 1024 * 1024
ATTN_BLK = 512
ATTN_HEADS = 4
LOG2_E = 1.4426950408889634
CAST_BLOCK_BYTES = 8 * 1024 * 1024
FFN_UP_TM = 1024
FFN_DOWN_TM = 512
FFN_DOWN_TN = 1024
FFN_TF = 512
BF16_SUBLANES = 16

BF16 = jnp.bfloat16
F32 = jnp.float32


def _rms(xf, gain):
    return xf * lax.rsqrt(jnp.mean(xf * xf, axis=-1, keepdims=True) + EPS) * gain


def _rope(blk, cos, sin_signed):
    lane = lax.broadcasted_iota(jnp.int32, blk.shape, 1)
    partner = jnp.where(lane % ROPE_DIM < ROPE_DIM // 2,
                        pltpu.roll(blk, LANES - ROPE_DIM // 2, 1),
                        pltpu.roll(blk, ROPE_DIM // 2, 1))
    return blk * cos + partner * sin_signed


def _cast_kernel(w_ref, o_ref):
    cols = w_ref.shape[-1]
    o_ref[:, :cols] = w_ref[...].astype(BF16)
    if o_ref.shape[-1] > cols:
        o_ref[:, cols:] = jnp.zeros((o_ref.shape[0], o_ref.shape[-1] - cols), BF16)


def _to_bf16(w, out_cols=None, layer=None):
    depth, rows, cols = w.shape
    out_cols = out_cols or cols
    tr = 16
    while rows % (2 * tr) == 0 and 2 * tr * cols * 4 <= CAST_BLOCK_BYTES:
        tr *= 2
    if layer is None:
        grid = (depth, rows // tr)
        in_spec = pl.BlockSpec((None, tr, cols), lambda l, i: (l, i, 0))
        out_spec = pl.BlockSpec((None, tr, out_cols), lambda l, i: (l, i, 0))
        out_shape = (depth, rows, out_cols)
    else:
        grid = (rows // tr,)
        in_spec = pl.BlockSpec((None, tr, cols), lambda i: (layer, i, 0))
        out_spec = pl.BlockSpec((tr, out_cols), lambda i: (i, 0))
        out_shape = (rows, out_cols)
    return pl.pallas_call(
        _cast_kernel,
        grid=grid,
        in_specs=[in_spec],
        out_specs=out_spec,
        out_shape=jax.ShapeDtypeStruct(out_shape, BF16),
        compiler_params=pltpu.CompilerParams(
            dimension_semantics=("arbitrary",) * len(grid), vmem_limit_bytes=VMEM_LIMIT),
        name="cast_bf16",
    )(w)


def _mod_kernel(c_ref, w_ref, b_ref, o_ref):
    ca = jax.nn.silu(c_ref[...]).astype(BF16)
    o_ref[...] = jnp.dot(ca, w_ref[...].astype(BF16), preferred_element_type=F32) + b_ref[...]


def _adaln_mod(c_pad, w_ada, b_ada):
    depth, d, n = w_ada.shape
    rows = c_pad.shape[0]
    tn = 1024
    return pl.pallas_call(
        _mod_kernel,
        grid=(depth, n // tn),
        in_specs=[
            pl.BlockSpec((rows, d), lambda l, j: (0, 0)),
            pl.BlockSpec((None, d, tn), lambda l, j: (l, 0, j)),
            pl.BlockSpec((None, 1, tn), lambda l, j: (l, 0, j)),
        ],
        out_specs=pl.BlockSpec((None, rows, tn), lambda l, j: (l, 0, j)),
        out_shape=jax.ShapeDtypeStruct((depth, rows, n), F32),
        compiler_params=pltpu.CompilerParams(
            dimension_semantics=("arbitrary", "arbitrary"), vmem_limit_bytes=VMEM_LIMIT),
        name="adaln_mod",
    )(c_pad, w_ada, b_ada.reshape(depth, 1, n))


def _rope_table_kernel(pos_ref, inv_ref, sign_ref, cos_ref, sin_ref):
    ang = pos_ref[...].astype(F32) * inv_ref[...]
    cos_ref[...] = jnp.cos(ang)
    sin_ref[...] = jnp.sin(ang) * sign_ref[...]


def _rope_tables(positions):
    t = positions.size
    tm = 2048
    inv = ROPE_BASE ** (-jnp.arange(0, ROPE_DIM, 2, dtype=F32) / ROPE_DIM)
    inv_l = jnp.tile(inv, LANES // (ROPE_DIM // 2)).reshape(1, LANES)
    sign = np.where(np.arange(LANES) % ROPE_DIM < ROPE_DIM // 2, -1.0, 1.0).astype(np.float32)
    return pl.pallas_call(
        _rope_table_kernel,
        grid=(t // tm,),
        in_specs=[
            pl.BlockSpec((tm, 1), lambda i: (i, 0)),
            pl.BlockSpec((1, LANES), lambda i: (0, 0)),
            pl.BlockSpec((1, LANES), lambda i: (0, 0)),
        ],
        out_specs=[pl.BlockSpec((tm, LANES), lambda i: (i, 0))] * 2,
        out_shape=[jax.ShapeDtypeStruct((t, LANES), F32)] * 2,
        compiler_params=pltpu.CompilerParams(dimension_semantics=("arbitrary",)),
        name="rope_tables",
    )(positions.reshape(t, 1), inv_l, jnp.asarray(sign).reshape(1, LANES))


def _ffn_up_kernel(*refs, mod_base, nchunks, side_cast):
    x0_ref, xnext_ref, mod_ref, modnext_ref, gain_ref, wg_ref, wu_ref = refs[:7]
    if side_cast:
        side_gu_ref, side_d_ref, a_ref, side_gu_out, side_d_out, h_even, h_odd = refs[7:]
    else:
        a_ref, h_even, h_odd = refs[7:]
    i = pl.program_id(0)
    j = pl.program_id(1)
    chunk = xnext_ref.shape[0]

    def norm_mod(xf, m_ref):
        y = _rms(xf, gain_ref[...])
        return (y * (1 + m_ref[mod_base + 1:mod_base + 2, :]) + m_ref[mod_base:mod_base + 1, :]
                ).astype(BF16)

    @pl.when((i == 0) & (j == 0))
    def _():
        h_even[...] = norm_mod(x0_ref[...], mod_ref)

    def step(h_ref, hnext_ref):
        zero_bits = None
        for c0 in range(0, wg_ref.shape[1], MXU_COLS):
            cols = slice(c0, c0 + MXU_COLS)
            wg = wg_ref[:, cols]
            if zero_bits is not None:
                wbits = pltpu.bitcast(wg, jnp.uint32)
                wbits = (wbits.reshape(-1, 8, MXU_COLS) | zero_bits[None]).reshape(wbits.shape)
                wg = pltpu.bitcast(wbits, BF16)
            g = jnp.dot(h_ref[...], wg, preferred_element_type=F32)
            u = jnp.dot(h_ref[...], wu_ref[:, cols], preferred_element_type=F32)
            a_ref[:, cols] = (jax.nn.silu(g) * u).astype(BF16)
            if c0 == 0:
                rows = pl.ds(pl.multiple_of(jnp.minimum(j, nchunks - 1) * chunk, chunk), chunk)
                hn = norm_mod(xnext_ref[...], modnext_ref)
                hnext_ref[rows, :] = hn
                bits = pltpu.bitcast(hn, jnp.uint32)
                bits = (bits >> 16) >> 16
                bits = functools.reduce(jnp.bitwise_or, [bits[r:r + 8] for r in range(0, bits.shape[0], 8)])
                zero_bits = functools.reduce(
                    jnp.bitwise_or, [bits[:, c:c + MXU_COLS] for c in range(0, bits.shape[1], MXU_COLS)])
                if side_cast:
                    side_gu_out[...] = side_gu_ref[...].astype(BF16)
                    side_d_out[...] = side_d_ref[...].astype(BF16)

    @pl.when(i % 2 == 0)
    def _():
        step(h_even, h_odd)

    @pl.when(i % 2 == 1)
    def _():
        step(h_odd, h_even)


def _ffn_down_kernel(*refs, mod_base, final_norm, n_side):
    a_ref, wd_ref, x_ref, mod_ref, fgain_ref = refs[:5]
    side_in = refs[5:5 + n_side]
    o_ref = refs[5 + n_side]
    side_out = refs[6 + n_side:]
    y = jnp.dot(a_ref[...], wd_ref[...], preferred_element_type=F32)
    out = x_ref[...] + 0.5 * mod_ref[mod_base + 2:mod_base + 3, :] * y
    if final_norm:
        out = _rms(out, fgain_ref[...])
    o_ref[...] = out
    for w_ref, wo_ref in zip(side_in, side_out):
        _cast_kernel(w_ref, wo_ref)


def _ffn(x, mod, gain, w_gu, w_down, final_gain, side, side_down=(), *, layer, mod_base,
         final_norm, seq):
    t, d = x.shape
    f = w_down.shape[0]
    tm, tf = FFN_UP_TM, FFN_TF
    nf = f // tf
    nt = t // tm
    nchunks = 8
    assert nchunks <= nf and tm % nchunks == 0
    chunk = tm // nchunks
    nxt = lambda i: jnp.minimum(i + 1, nt - 1)
    in_specs = [
        pl.BlockSpec((tm, d), lambda i, j: (0, 0), pipeline_mode=pl.Buffered(1)),
        pl.BlockSpec((chunk, d), lambda i, j: (nxt(i) * nchunks + jnp.minimum(j, nchunks - 1), 0)),
        pl.BlockSpec((None, None, N_MOD, d), lambda i, j: (layer, i // (seq // tm), 0, 0)),
        pl.BlockSpec((None, None, N_MOD, d), lambda i, j: (layer, nxt(i) // (seq // tm), 0, 0)),
        pl.BlockSpec((1, d), lambda i, j: (0, 0)),
        pl.BlockSpec((d, tf), lambda i, j: (0, j)),
        pl.BlockSpec((d, tf), lambda i, j: (0, nf + j)),
    ]
    operands = [x, x, mod, mod, gain.reshape(1, d), w_gu, w_gu]
    out_specs = [pl.BlockSpec((tm, tf), lambda i, j: (i, j))]
    out_shape = [jax.ShapeDtypeStruct((t, f), BF16)]
    if side is not None:
        side_layer = side[2]
        for w in side[:2]:
            _, rows, cols = w.shape
            rps = -(-rows // (BF16_SUBLANES * nt * nf)) * BF16_SUBLANES
            assert rows % rps == 0
            blk = lambda i, j, last=rows // rps - 1: jnp.minimum(i * nf + j, last)
            in_specs.append(pl.BlockSpec((None, rps, cols),
                                         lambda i, j, blk=blk: (side_layer, blk(i, j), 0)))
            operands.append(w)
            out_specs.append(pl.BlockSpec((rps, cols), lambda i, j, blk=blk: (blk(i, j), 0)))
            out_shape.append(jax.ShapeDtypeStruct((rows, cols), BF16))
    outs = pl.pallas_call(
        functools.partial(_ffn_up_kernel, mod_base=mod_base, nchunks=nchunks,
                          side_cast=side is not None),
        grid=(nt, nf),
        in_specs=in_specs,
        out_specs=out_specs,
        out_shape=out_shape,
        scratch_shapes=[pltpu.VMEM((tm, d), BF16), pltpu.VMEM((tm, d), BF16)],
        compiler_params=pltpu.CompilerParams(
            dimension_semantics=("arbitrary", "arbitrary"), vmem_limit_bytes=VMEM_LIMIT),
        name="ffn_gate_up",
    )(*operands)

    tm2, tn = (FFN_DOWN_TM // 2, d) if final_norm else (FFN_DOWN_TM, FFN_DOWN_TN)
    wd_mode = dict(pipeline_mode=pl.Buffered(1)) if final_norm else {}
    nrow = t // tm2
    in_specs = [
        pl.BlockSpec((tm2, f), lambda n, i: (i, 0)),
        pl.BlockSpec((f, tn), lambda n, i: (0, n), **wd_mode),
        pl.BlockSpec((tm2, tn), lambda n, i: (i, n)),
        pl.BlockSpec((None, None, N_MOD, tn), lambda n, i: (layer, i // (seq // tm2), 0, n)),
        pl.BlockSpec((1, tn), lambda n, i: (0, n)),
    ]
    operands = [outs[0], w_down, x, mod, final_gain.reshape(1, d)]
    out_specs = [pl.BlockSpec((tm2, tn), lambda n, i: (i, n))]
    out_shape = [jax.ShapeDtypeStruct((t, d), F32)]
    for w, out_cols in side_down:
        rows, cols = w.shape
        rps = -(-rows // (BF16_SUBLANES * (d // tn) * nrow)) * BF16_SUBLANES
        assert rows % rps == 0
        blk = lambda n, i, last=rows // rps - 1: jnp.minimum(n * nrow + i, last)
        in_specs.append(pl.BlockSpec((rps, cols), lambda n, i, blk=blk: (blk(n, i), 0)))
        operands.append(w)
        out_specs.append(pl.BlockSpec((rps, out_cols), lambda n, i, blk=blk: (blk(n, i), 0)))
        out_shape.append(jax.ShapeDtypeStruct((rows, out_cols), BF16))
    down = pl.pallas_call(
        functools.partial(_ffn_down_kernel, mod_base=mod_base, final_norm=final_norm,
                          n_side=len(side_down)),
        grid=(d // tn, nrow),
        in_specs=in_specs,
        out_specs=out_specs,
        out_shape=out_shape,
        compiler_params=pltpu.CompilerParams(
            dimension_semantics=("arbitrary", "arbitrary"), vmem_limit_bytes=VMEM_LIMIT),
        name="ffn_down",
    )(*operands)
    return down[0], tuple(outs[1:]), tuple(down[1:])


def _inproj_kernel(x_ref, mod_ref, gain_ref, cos_ref, sin_ref, win_ref, qg_ref, wuq_ref,
                   kvg_ref, wukv_ref, qk_ref, v_ref, g_ref, qm_ref, km_ref, vm_ref):
    y = _rms(x_ref[...], gain_ref[...])
    h = (y * (1 + mod_ref[4:5, :]) + mod_ref[3:4, :]).astype(BF16)
    cos = cos_ref[...]
    sin = sin_ref[...]

    def proj(c0, c1):
        return jnp.dot(h, win_ref[:, c0:c1], preferred_element_type=F32)

    p_cq = proj(IN_CQ0, IN_CKV0)
    tail = proj(IN_CKV0, IN_COLS_PAD)

    half = IN_QK // 2
    for c0, mult in ((0, 1.0), (half, RET_DK ** -0.5)):
        p = proj(c0, c0 + half)
        for t in range(half // LANES):
            r = _rope(p[:, t * LANES:(t + 1) * LANES], cos, sin)
            qk_ref[:, c0 + t * LANES:c0 + (t + 1) * LANES] = (r * mult).astype(BF16)

    cq = _rms(p_cq, qg_ref[...]).astype(BF16)
    q = jnp.dot(cq, wuq_ref[...], preferred_element_type=F32)
    qscale = (MLA_NOPE + MLA_ROPE) ** -0.5 * LOG2_E
    for hd in range(MLA_HEADS):
        b0 = hd * MLA_QK_PAD
        qm_ref[:, b0:b0 + LANES] = (q[:, b0:b0 + LANES] * qscale).astype(BF16)
        r = _rope(q[:, b0 + LANES:b0 + 2 * LANES], cos, sin)
        qm_ref[:, b0 + LANES:b0 + 2 * LANES] = (r * qscale).astype(BF16)

    ckv = _rms(tail[:, :MLA_KV_RANK], kvg_ref[...]).astype(BF16)
    kpe = _rope(tail[:, MLA_KV_RANK:], cos, sin).astype(BF16)
    kv = jnp.dot(ckv, wukv_ref[...], preferred_element_type=F32)
    for hd in range(MLA_HEADS):
        b0 = hd * (MLA_NOPE + MLA_DV)
        km_ref[:, hd * MLA_QK_PAD:hd * MLA_QK_PAD + LANES] = kv[:, b0:b0 + MLA_NOPE].astype(BF16)
        km_ref[:, hd * MLA_QK_PAD + LANES:(hd + 1) * MLA_QK_PAD] = kpe
        vm_ref[hd * MLA_DV:(hd + 1) * MLA_DV, :] = (
            kv[:, b0 + MLA_NOPE:b0 + MLA_NOPE + MLA_DV].T.astype(BF16))

    v_ref[...] = proj(IN_V0, IN_G0).astype(BF16)
    g_ref[...] = proj(IN_G0, IN_CQ0).astype(BF16)


def _inproj(x, mod, gain, cos, sin, w_in, q_gain, w_uq, kv_gain, w_ukv, *, layer, seq):
    t, d = x.shape
    tm = 256
    const = lambda i: (0, 0)
    row = lambda i: (i, 0)

    def resident(w):
        return pl.BlockSpec((None,) + w.shape[1:], lambda i: (layer, 0, 0),
                            pipeline_mode=pl.Buffered(1))

    widths = [IN_QK, RET_WIDTH, RET_WIDTH, MLA_HEADS * MLA_QK_PAD, MLA_HEADS * MLA_QK_PAD]
    tpb = seq // tm
    sub = ATTN_BLK // tm
    vt_spec = pl.BlockSpec((None, None, MLA_WIDTH, tm),
                           lambda i: (i // tpb, (i % tpb) // sub, 0, i % sub))
    vt_shape = jax.ShapeDtypeStruct((t // seq, seq // ATTN_BLK, MLA_WIDTH, ATTN_BLK), BF16)
    return pl.pallas_call(
        _inproj_kernel,
        grid=(t // tm,),
        in_specs=[
            pl.BlockSpec((tm, d), row),
            pl.BlockSpec((None, None, N_MOD, d), lambda i: (layer, i // (seq // tm), 0, 0)),
            pl.BlockSpec((1, d), const),
            pl.BlockSpec((tm, LANES), row),
            pl.BlockSpec((tm, LANES), row),
            resident(w_in),
            pl.BlockSpec((1, MLA_Q_RANK), const),
            resident(w_uq),
            pl.BlockSpec((1, MLA_KV_RANK), const),
            resident(w_ukv),
        ],
        out_specs=[pl.BlockSpec((tm, w), row) for w in widths] + [vt_spec],
        out_shape=[jax.ShapeDtypeStruct((t, w), BF16) for w in widths] + [vt_shape],
        compiler_params=pltpu.CompilerParams(
            dimension_semantics=("arbitrary",), vmem_limit_bytes=VMEM_LIMIT),
        name="mixer_inproj",
    )(x, mod, gain.reshape(1, d), cos, sin, w_in, q_gain.reshape(1, -1), w_uq,
      kv_gain.reshape(1, -1), w_ukv)


def _retention_kernel(qk_ref, v_ref, g_ref, intra_ref, qdec_ref, kdec_ref, cdec_ref, hmask_ref,
                      gain_ref, o_ref, state_sc, *, chunks):
    @pl.when(pl.program_id(1) == 0)
    def _():
        state_sc[...] = jnp.zeros_like(state_sc)

    kbase = RET_HEADS * RET_DK

    def chunk_body(c, carry):
        rows = pl.ds(pl.multiple_of(c * RET_CHUNK, RET_CHUNK), RET_CHUNK)
        for pair in range(RET_HEADS // 2):
            qf = qk_ref[rows, pair * LANES:(pair + 1) * LANES].astype(F32)
            kp = qk_ref[rows, kbase + pair * LANES:kbase + (pair + 1) * LANES]
            kdt = (kp.astype(F32) * kdec_ref[pair]).T.astype(BF16)
            for hh in range(2):
                hd = 2 * pair + hh
                cols = slice(hd * RET_DV, (hd + 1) * RET_DV)
                vh = v_ref[rows, cols]
                qm = (qf * hmask_ref[hh]).astype(BF16)
                s = lax.dot_general(qm, kp, (((1,), (1,)), ((), ())), preferred_element_type=F32)
                intra = jnp.dot((s * intra_ref[hd]).astype(BF16), vh, preferred_element_type=F32)
                qd = (qf * qdec_ref[hd]).astype(BF16)
                state = state_sc[hd]
                cross = jnp.dot(qd, state.astype(BF16), preferred_element_type=F32)
                state_sc[hd] = state * cdec_ref[hd] + jnp.dot(kdt, vh, preferred_element_type=F32)
                yh = _rms(intra + cross, gain_ref[:, cols])
                o_ref[rows, cols] = (jax.nn.silu(g_ref[rows, cols].astype(F32)) * yh).astype(BF16)
        return carry

    lax.fori_loop(0, chunks, chunk_body, 0)


def _retention_consts():
    c = RET_CHUNK
    log_g = jnp.log1p(-(2.0 ** (-5.0 - jnp.arange(RET_HEADS, dtype=F32))))
    idx = jnp.arange(c, dtype=F32)
    rel = idx[:, None] - idx[None, :]
    intra = jnp.where(rel >= 0, jnp.exp(log_g[:, None, None] * jnp.maximum(rel, 0.0)), 0.0)
    q_decay = jnp.exp(log_g[:, None] * (idx + 1.0))
    k_decay = jnp.exp(log_g[:, None] * (c - 1.0 - idx))
    chunk_decay = jnp.exp(log_g * c)
    own = (np.arange(LANES)[None, :] // RET_DK == np.arange(2)[:, None]).astype(np.float32)
    hmask = jnp.asarray(own).reshape(2, 1, LANES)
    qdec = q_decay[:, :, None] * jnp.asarray(own)[jnp.arange(RET_HEADS) % 2][:, None, :]
    kdec = jnp.repeat(k_decay.reshape(RET_HEADS // 2, 2, c), RET_DK, axis=1).transpose(0, 2, 1)
    cdec = jnp.broadcast_to(chunk_decay[:, None, None], (RET_HEADS, 1, LANES))
    return intra, qdec, kdec, cdec, hmask


def _retention(qk, v, g, gain, *, batch, seq):
    tb = 1024
    chunks = tb // RET_CHUNK
    nb = seq // tb
    intra, qdec, kdec, cdec, hmask = _retention_consts()
    row = lambda b, i: (b * nb + i, 0)
    c3 = lambda b, i: (0, 0, 0)
    return pl.pallas_call(
        functools.partial(_retention_kernel, chunks=chunks),
        grid=(batch, nb),
        in_specs=[
            pl.BlockSpec((tb, IN_QK), row),
            pl.BlockSpec((tb, RET_WIDTH), row),
            pl.BlockSpec((tb, RET_WIDTH), row),
            pl.BlockSpec(intra.shape, c3),
            pl.BlockSpec(qdec.shape, c3),
            pl.BlockSpec(kdec.shape, c3),
            pl.BlockSpec(cdec.shape, c3),
            pl.BlockSpec(hmask.shape, c3),
            pl.BlockSpec((1, RET_WIDTH), lambda b, i: (0, 0)),
        ],
        out_specs=pl.BlockSpec((tb, RET_WIDTH), row),
        out_shape=jax.ShapeDtypeStruct((batch * seq, RET_WIDTH), BF16),
        scratch_shapes=[pltpu.VMEM((RET_HEADS, LANES, RET_DV), F32)],
        compiler_params=pltpu.CompilerParams(
            dimension_semantics=("arbitrary", "arbitrary"), vmem_limit_bytes=VMEM_LIMIT),
        name="retention",
    )(qk, v, g, intra, qdec, kdec, cdec, hmask, gain.reshape(1, RET_WIDTH))


def _attn_kernel(q_ref, qnext_ref, k_ref, vt_ref, o_ref, s_sc, acc_sc, m_sc, l_sc):
    qi = pl.program_id(2)
    blk = ATTN_BLK
    acc_sc[...] = jnp.zeros_like(acc_sc)

    def scores(ki, slot, queries=q_ref):
        rows = pl.ds(pl.multiple_of(ki * blk, blk), blk)
        for hd in range(ATTN_HEADS):
            cols = slice(hd * MLA_QK_PAD, (hd + 1) * MLA_QK_PAD)
            s_sc[slot, hd] = lax.dot_general(k_ref[rows, cols], queries[:, cols],
                                             (((1,), (1,)), ((), ())), preferred_element_type=F32)

    def update(ki, slot, stats, masked):
        new_stats = []
        for hd in range(ATTN_HEADS):
            m_prev, l_prev = stats[hd]
            s = s_sc[slot, hd]
            if masked:
                kpos = lax.broadcasted_iota(jnp.int32, s.shape, 0)
                qpos = lax.broadcasted_iota(jnp.int32, s.shape, 1)
                s = jnp.where(kpos <= qpos, s, -jnp.inf)
            m_new = jnp.maximum(m_prev, jnp.max(s, axis=0, keepdims=True))
            p = jnp.exp2(s - m_new)
            alpha = jnp.exp2(m_prev - m_new)
            l_new = alpha * l_prev + jnp.sum(p, axis=0, keepdims=True)
            vt = vt_ref[ki, hd * MLA_DV:(hd + 1) * MLA_DV, :]
            acc_sc[hd] = alpha * acc_sc[hd] + jnp.dot(vt, p.astype(BF16), preferred_element_type=F32)
            new_stats.append((m_new, l_new))
        return tuple(new_stats)

    def load_stats():
        return tuple((m_sc[hd], l_sc[hd]) for hd in range(ATTN_HEADS))

    def store_stats(stats):
        for hd in range(ATTN_HEADS):
            m_sc[hd], l_sc[hd] = stats[hd]

    @pl.when(qi == 0)
    def _():
        scores(0, 0)

    def block_pair(kp, stats):
        scores(2 * kp + 1, 1)
        stats = update(2 * kp, 0, stats, False)
        scores(2 * kp + 2, 0)
        stats = update(2 * kp + 1, 1, stats, False)
        return stats

    init = tuple((jnp.full((1, blk), -jnp.inf, F32), jnp.zeros((1, blk), F32))
                 for _ in range(ATTN_HEADS))
    store_stats(lax.fori_loop(0, qi // 2, block_pair, init))

    @pl.when(qi % 2 == 1)
    def _():
        stats = update(qi - 1, 0, load_stats(), False)
        scores(qi, 1)
        store_stats(stats)

    stats = update(qi, qi % 2, load_stats(), True)
    scores(0, 0, qnext_ref)
    for hd in range(ATTN_HEADS):
        out_t = acc_sc[hd] / stats[hd][1]
        o_ref[:, hd * MLA_DV:(hd + 1) * MLA_DV] = out_t.T.astype(BF16)


def _attention(qm, km, vt, *, batch, seq):
    blk = ATTN_BLK
    nq = seq // blk
    qm = qm.reshape(batch, seq, -1)
    km = km.reshape(batch, seq, -1)
    out = pl.pallas_call(
        _attn_kernel,
        grid=(batch, MLA_HEADS // ATTN_HEADS, nq),
        in_specs=[
            pl.BlockSpec((None, blk, ATTN_HEADS * MLA_QK_PAD), lambda b, h, i: (b, i, h)),
            pl.BlockSpec((None, blk, ATTN_HEADS * MLA_QK_PAD),
                         lambda b, h, i: (b, jnp.minimum(i + 1, nq - 1), h)),
            pl.BlockSpec((None, seq, ATTN_HEADS * MLA_QK_PAD), lambda b, h, i: (b, 0, h)),
            pl.BlockSpec((None, nq, ATTN_HEADS * MLA_DV, blk), lambda b, h, i: (b, 0, h, 0)),
        ],
        out_specs=pl.BlockSpec((None, blk, ATTN_HEADS * MLA_DV), lambda b, h, i: (b, i, h)),
        out_shape=jax.ShapeDtypeStruct((batch, seq, MLA_WIDTH), BF16),
        scratch_shapes=[pltpu.VMEM((2, ATTN_HEADS, blk, blk), F32),
                        pltpu.VMEM((ATTN_HEADS, MLA_DV, blk), F32),
                        pltpu.VMEM((ATTN_HEADS, 1, blk), F32), pltpu.VMEM((ATTN_HEADS, 1, blk), F32)],
        compiler_params=pltpu.CompilerParams(
            dimension_semantics=("arbitrary", "arbitrary", "arbitrary"),
            vmem_limit_bytes=VMEM_LIMIT),
        name="mla_attention",
    )(qm, qm, km, vt)
    return out.reshape(batch * seq, MLA_WIDTH)


def _outproj_kernel(x_ref, mod_ref, yr_ref, ym_ref, w_ref, o_ref):
    y = jnp.dot(yr_ref[...], w_ref[:RET_WIDTH, :], preferred_element_type=F32)
    y += jnp.dot(ym_ref[...], w_ref[RET_WIDTH:, :], preferred_element_type=F32)
    o_ref[...] = x_ref[...] + mod_ref[5:6, :] * y


def _outproj(x, mod, y_r, y_m, w_out, *, layer, seq):
    t, d = x.shape
    tm = 512
    row = lambda i: (i, 0)
    return pl.pallas_call(
        _outproj_kernel,
        grid=(t // tm,),
        in_specs=[
            pl.BlockSpec((tm, d), row),
            pl.BlockSpec((None, None, N_MOD, d), lambda i: (layer, i // (seq // tm), 0, 0)),
            pl.BlockSpec((tm, RET_WIDTH), row),
            pl.BlockSpec((tm, MLA_WIDTH), row),
            pl.BlockSpec((None,) + w_out.shape[1:], lambda i: (layer, 0, 0),
                         pipeline_mode=pl.Buffered(1)),
        ],
        out_specs=pl.BlockSpec((tm, d), row),
        out_shape=jax.ShapeDtypeStruct((t, d), F32),
        compiler_params=pltpu.CompilerParams(
            dimension_semantics=("arbitrary",), vmem_limit_bytes=VMEM_LIMIT),
        name="mixer_outproj",
    )(x, mod, y_r, y_m, w_out)


def kernel(x, c, positions, w_ada, b_ada, norm_ffn1, ffn1_w_gu, ffn1_w_down, norm_mix, w_in,
           ret_norm_g, q_norm_g, w_uq, kv_norm_g, w_ukv, w_out, norm_ffn2, ffn2_w_gu,
           ffn2_w_down, final_norm):
    batch, seq, d = x.shape
    depth = w_ada.shape[0]
    xt = x.reshape(batch * seq, d)

    c_pad = jnp.pad(c, ((0, 8 - batch), (0, 0)))
    mod = _adaln_mod(c_pad, w_ada, b_ada)[:, :batch].reshape(depth, batch, N_MOD, d)
    cos, sin = _rope_tables(positions)

    w_uq_b = jnp.pad(w_uq.reshape(depth, MLA_Q_RANK, MLA_HEADS, MLA_NOPE + MLA_ROPE),
                     ((0, 0), (0, 0), (0, 0), (0, MLA_QK_PAD - MLA_NOPE - MLA_ROPE))
                     ).reshape(depth, MLA_Q_RANK, MLA_HEADS * MLA_QK_PAD).astype(BF16)
    ffn_f32 = [(ffn1_w_gu, ffn1_w_down), (ffn2_w_gu, ffn2_w_down)]
    ffn_gain = [norm_ffn1, norm_ffn2]
    ffn_w = (_to_bf16(ffn1_w_gu, layer=0), _to_bf16(ffn1_w_down, layer=0))
    mixer_f32 = [(w.reshape(-1, w.shape[-1]), cols)
                 for w, cols in ((w_in, IN_COLS_PAD), (w_out, d), (w_ukv, w_ukv.shape[-1]))]

    def ffn(xt, weights, l, which):
        first = l == 0 and which == 0
        last = l == depth - 1 and which == 1
        nxt_l, nxt_which = (l, 1) if which == 0 else (l + 1, 0)
        side = None if last else ffn_f32[nxt_which] + (nxt_l,)
        return _ffn(xt, mod, ffn_gain[which][l], weights[0], weights[1], final_norm, side,
                    mixer_f32 if first else (), layer=l, mod_base=6 * which, final_norm=last,
                    seq=seq)

    xt, ffn_w, mixer_b = ffn(xt, ffn_w, 0, 0)
    w_in_b, w_out_b, w_ukv_b = (w.reshape(depth, -1, w.shape[-1]) for w in mixer_b)
    for l in range(depth):
        if l > 0:
            xt, ffn_w, _ = ffn(xt, ffn_w, l, 0)
        qk, v_r, g_r, qm, km, vt = _inproj(xt, mod, norm_mix[l], cos, sin, w_in_b, q_norm_g[l],
                                           w_uq_b, kv_norm_g[l], w_ukv_b, layer=l, seq=seq)
        y_r = _retention(qk, v_r, g_r, ret_norm_g[l], batch=batch, seq=seq)
        y_m = _attention(qm, km, vt, batch=batch, seq=seq)
        xt = _outproj(xt, mod, y_r, y_m, w_out_b, layer=l, seq=seq)
        xt, ffn_w, _ = ffn(xt, ffn_w, l, 1)
    return xt.reshape(batch, seq, d)
```

```python
import functools

import jax
import jax.numpy as jnp
import numpy as np
from jax import lax
from jax.experimental import pallas as pl
from jax.experimental.pallas import tpu as pltpu

D_MODEL = 2048
DEPTH = 2
RET_HEADS = 8
RET_DK = 64
RET_DV = 128
RET_CHUNK = 128
MLA_HEADS = 8
MLA_Q_RANK = 512
MLA_KV_RANK = 256
MLA_NOPE = 128
MLA_ROPE = 64
MLA_DV = 128
RET_WIDTH = RET_HEADS * RET_DV
MLA_WIDTH = MLA_HEADS * MLA_DV
ROPE_DIM = 64
ROPE_BASE = 10000.0
EPS = 1e-6
N_MOD = 9

LANES = 128
MXU_COLS = 256
MLA_QK_PAD = 256
IN_QK = 2 * RET_HEADS * RET_DK
IN_V0 = IN_QK
IN_G0 = IN_V0 + RET_WIDTH
IN_CQ0 = IN_G0 + RET_WIDTH
IN_CKV0 = IN_CQ0 + MLA_Q_RANK
IN_KPE0 = IN_CKV0 + MLA_KV_RANK
IN_COLS_PAD = IN_KPE0 + LANES

VMEM_LIMIT = 56 * 1024 * 1024
ATTN_BLK = 512
ATTN_HEADS = 4
LOG2_E = 1.4426950408889634
CAST_BLOCK_BYTES = 8 * 1024 * 1024
FFN_UP_TM = 1024
FFN_DOWN_TM = 512
FFN_DOWN_TN = 1024
FFN_TF = 512
BF16_SUBLANES = 16
FFN_NEXT_CHUNKS = 8
MOD_TN = 1024
ROPE_TM = 2048
INPROJ_TM = 256
OUTPROJ_TM = 512
RET_TB = 1024

BF16 = jnp.bfloat16
F32 = jnp.float32


def _rms(xf, gain):
    return xf * lax.rsqrt(jnp.mean(xf * xf, axis=-1, keepdims=True) + EPS) * gain


def _rope(blk, cos, sin_signed):
    lane = lax.broadcasted_iota(jnp.int32, blk.shape, 1)
    partner = jnp.where(lane % ROPE_DIM < ROPE_DIM // 2,
                        pltpu.roll(blk, LANES - ROPE_DIM // 2, 1),
                        pltpu.roll(blk, ROPE_DIM // 2, 1))
    return blk * cos + partner * sin_signed


def _cast_kernel(w_ref, o_ref):
    cols = w_ref.shape[-1]
    o_ref[:, :cols] = w_ref[...].astype(BF16)
    if o_ref.shape[-1] > cols:
        o_ref[:, cols:] = jnp.zeros((o_ref.shape[0], o_ref.shape[-1] - cols), BF16)


def _to_bf16(w, out_cols=None, layer=None):
    depth, rows, cols = w.shape
    out_cols = out_cols or cols
    tr = 16
    while rows % (2 * tr) == 0 and 2 * tr * cols * 4 <= CAST_BLOCK_BYTES:
        tr *= 2
    if layer is None:
        grid = (depth, rows // tr)
        in_spec = pl.BlockSpec((None, tr, cols), lambda l, i: (l, i, 0))
        out_spec = pl.BlockSpec((None, tr, out_cols), lambda l, i: (l, i, 0))
        out_shape = (depth, rows, out_cols)
    else:
        grid = (rows // tr,)
        in_spec = pl.BlockSpec((None, tr, cols), lambda i: (layer, i, 0))
        out_spec = pl.BlockSpec((tr, out_cols), lambda i: (i, 0))
        out_shape = (rows, out_cols)
    return pl.pallas_call(
        _cast_kernel,
        grid=grid,
        in_specs=[in_spec],
        out_specs=out_spec,
        out_shape=jax.ShapeDtypeStruct(out_shape, BF16),
        compiler_params=pltpu.CompilerParams(
            dimension_semantics=("arbitrary",) * len(grid), vmem_limit_bytes=VMEM_LIMIT),
        name="cast_bf16",
    )(w)


def _mod_kernel(c_ref, w_ref, b_ref, o_ref):
    ca = jax.nn.silu(c_ref[...]).astype(BF16)
    o_ref[...] = jnp.dot(ca, w_ref[...].astype(BF16), preferred_element_type=F32) + b_ref[...]


def _adaln_mod(c_pad, w_ada, b_ada):
    depth, d, n = w_ada.shape
    rows = c_pad.shape[0]
    tn = MOD_TN
    return pl.pallas_call(
        _mod_kernel,
        grid=(depth, n // tn),
        in_specs=[
            pl.BlockSpec((rows, d), lambda l, j: (0, 0)),
            pl.BlockSpec((None, d, tn), lambda l, j: (l, 0, j)),
            pl.BlockSpec((None, 1, tn), lambda l, j: (l, 0, j)),
        ],
        out_specs=pl.BlockSpec((None, rows, tn), lambda l, j: (l, 0, j)),
        out_shape=jax.ShapeDtypeStruct((depth, rows, n), F32),
        compiler_params=pltpu.CompilerParams(
            dimension_semantics=("arbitrary", "arbitrary"), vmem_limit_bytes=VMEM_LIMIT),
        name="adaln_mod",
    )(c_pad, w_ada, b_ada.reshape(depth, 1, n))


def _rope_table_kernel(pos_ref, inv_ref, sign_ref, cos_ref, sin_ref):
    ang = pos_ref[...].astype(F32) * inv_ref[...]
    cos_ref[...] = jnp.cos(ang)
    sin_ref[...] = jnp.sin(ang) * sign_ref[...]


def _rope_tables(positions):
    t = positions.size
    tm = ROPE_TM
    inv = ROPE_BASE ** (-jnp.arange(0, ROPE_DIM, 2, dtype=F32) / ROPE_DIM)
    inv_l = jnp.tile(inv, LANES // (ROPE_DIM // 2)).reshape(1, LANES)
    sign = np.where(np.arange(LANES) % ROPE_DIM < ROPE_DIM // 2, -1.0, 1.0).astype(np.float32)
    return pl.pallas_call(
        _rope_table_kernel,
        grid=(t // tm,),
        in_specs=[
            pl.BlockSpec((tm, 1), lambda i: (i, 0)),
            pl.BlockSpec((1, LANES), lambda i: (0, 0)),
            pl.BlockSpec((1, LANES), lambda i: (0, 0)),
        ],
        out_specs=[pl.BlockSpec((tm, LANES), lambda i: (i, 0))] * 2,
        out_shape=[jax.ShapeDtypeStruct((t, LANES), F32)] * 2,
        compiler_params=pltpu.CompilerParams(dimension_semantics=("arbitrary",)),
        name="rope_tables",
    )(positions.reshape(t, 1), inv_l, jnp.asarray(sign).reshape(1, LANES))


def _ffn_up_kernel(*refs, mod_base, nchunks, side_cast):
    x0_ref, xnext_ref, mod_ref, modnext_ref, gain_ref, wg_ref, wu_ref = refs[:7]
    if side_cast:
        side_gu_ref, side_d_ref, a_ref, side_gu_out, side_d_out, h_even, h_odd = refs[7:]
    else:
        a_ref, h_even, h_odd = refs[7:]
    i = pl.program_id(0)
    j = pl.program_id(1)
    chunk = xnext_ref.shape[0]

    def norm_mod(xf, m_ref):
        y = _rms(xf, gain_ref[...])
        return (y * (1 + m_ref[mod_base + 1:mod_base + 2, :]) + m_ref[mod_base:mod_base + 1, :]
                ).astype(BF16)

    @pl.when((i == 0) & (j == 0))
    def _():
        h_even[...] = norm_mod(x0_ref[...], mod_ref)

    def step(h_ref, hnext_ref):
        row0 = 0
        for c0 in range(0, wg_ref.shape[1], MXU_COLS):
            cols = slice(c0, c0 + MXU_COLS)
            h = h_ref[pl.ds(row0, h_ref.shape[0]), :]
            g = jnp.dot(h, wg_ref[:, cols], preferred_element_type=F32)
            u = jnp.dot(h, wu_ref[:, cols], preferred_element_type=F32)
            a_ref[:, cols] = (jax.nn.silu(g) * u).astype(BF16)
            if c0 == 0:
                rows = pl.ds(pl.multiple_of(jnp.minimum(j, nchunks - 1) * chunk, chunk), chunk)
                hn = norm_mod(xnext_ref[...], modnext_ref)
                hnext_ref[rows, :] = hn
                bits = pltpu.bitcast(hn, jnp.uint32)
                bits = (bits >> 16) >> 16
                bits = functools.reduce(jnp.bitwise_or, [bits[r:r + 8] for r in range(0, bits.shape[0], 8)])
                bits = functools.reduce(
                    jnp.bitwise_or, [bits[:, c:c + LANES] for c in range(0, bits.shape[1], LANES)])
                row0 = pl.multiple_of(jnp.max(bits.astype(jnp.int32)), BF16_SUBLANES)
                if side_cast:
                    side_gu_out[...] = side_gu_ref[...].astype(BF16)
                    side_d_out[...] = side_d_ref[...].astype(BF16)

    @pl.when(i % 2 == 0)
    def _():
        step(h_even, h_odd)

    @pl.when(i % 2 == 1)
    def _():
        step(h_odd, h_even)


def _ffn_down_kernel(*refs, mod_base, final_norm, n_side):
    a_ref, wd_ref, x_ref, mod_ref, fgain_ref = refs[:5]
    side_in = refs[5:5 + n_side]
    o_ref = refs[5 + n_side]
    side_out = refs[6 + n_side:]
    y = jnp.dot(a_ref[...], wd_ref[...], preferred_element_type=F32)
    out = x_ref[...] + 0.5 * mod_ref[mod_base + 2:mod_base + 3, :] * y
    if final_norm:
        out = _rms(out, fgain_ref[...])
    o_ref[...] = out
    for w_ref, wo_ref in zip(side_in, side_out):
        _cast_kernel(w_ref, wo_ref)


def _ffn(x, mod, gain, w_gu, w_down, final_gain, side, side_down=(), *, layer, mod_base,
         final_norm, seq):
    t, d = x.shape
    f = w_down.shape[0]
    tm, tf = FFN_UP_TM, FFN_TF
    nf = f // tf
    nt = t // tm
    nchunks = FFN_NEXT_CHUNKS
    assert nchunks <= nf and tm % nchunks == 0
    chunk = tm // nchunks
    nxt = lambda i: jnp.minimum(i + 1, nt - 1)
    in_specs = [
        pl.BlockSpec((tm, d), lambda i, j: (0, 0), pipeline_mode=pl.Buffered(1)),
        pl.BlockSpec((chunk, d), lambda i, j: (nxt(i) * nchunks + jnp.minimum(j, nchunks - 1), 0)),
        pl.BlockSpec((None, None, N_MOD, d), lambda i, j: (layer, i // (seq // tm), 0, 0)),
        pl.BlockSpec((None, None, N_MOD, d), lambda i, j: (layer, nxt(i) // (seq // tm), 0, 0)),
        pl.BlockSpec((1, d), lambda i, j: (0, 0)),
        pl.BlockSpec((d, tf), lambda i, j: (0, j)),
        pl.BlockSpec((d, tf), lambda i, j: (0, nf + j)),
    ]
    operands = [x, x, mod, mod, gain.reshape(1, d), w_gu, w_gu]
    out_specs = [pl.BlockSpec((tm, tf), lambda i, j: (i, j))]
    out_shape = [jax.ShapeDtypeStruct((t, f), BF16)]
    if side is not None:
        side_layer = side[2]
        for w in side[:2]:
            _, rows, cols = w.shape
            rps = -(-rows // (BF16_SUBLANES * nt * nf)) * BF16_SUBLANES
            assert rows % rps == 0
            blk = lambda i, j, last=rows // rps - 1: jnp.minimum(i * nf + j, last)
            in_specs.append(pl.BlockSpec((None, rps, cols),
                                         lambda i, j, blk=blk: (side_layer, blk(i, j), 0)))
            operands.append(w)
            out_specs.append(pl.BlockSpec((rps, cols), lambda i, j, blk=blk: (blk(i, j), 0)))
            out_shape.append(jax.ShapeDtypeStruct((rows, cols), BF16))
    outs = pl.pallas_call(
        functools.partial(_ffn_up_kernel, mod_base=mod_base, nchunks=nchunks,
                          side_cast=side is not None),
        grid=(nt, nf),
        in_specs=in_specs,
        out_specs=out_specs,
        out_shape=out_shape,
        scratch_shapes=[pltpu.VMEM((tm, d), BF16), pltpu.VMEM((tm, d), BF16)],
        compiler_params=pltpu.CompilerParams(
            dimension_semantics=("arbitrary", "arbitrary"), vmem_limit_bytes=VMEM_LIMIT),
        name="ffn_gate_up",
    )(*operands)

    tm2, tn = (FFN_DOWN_TM // 2, d) if final_norm else (FFN_DOWN_TM, FFN_DOWN_TN)
    wd_mode = dict(pipeline_mode=pl.Buffered(1)) if final_norm else {}
    nrow = t // tm2
    in_specs = [
        pl.BlockSpec((tm2, f), lambda n, i: (i, 0)),
        pl.BlockSpec((f, tn), lambda n, i: (0, n), **wd_mode),
        pl.BlockSpec((tm2, tn), lambda n, i: (i, n)),
        pl.BlockSpec((None, None, N_MOD, tn), lambda n, i: (layer, i // (seq // tm2), 0, n)),
        pl.BlockSpec((1, tn), lambda n, i: (0, n)),
    ]
    operands = [outs[0], w_down, x, mod, final_gain.reshape(1, d)]
    out_specs = [pl.BlockSpec((tm2, tn), lambda n, i: (i, n))]
    out_shape = [jax.ShapeDtypeStruct((t, d), F32)]
    for w, out_cols in side_down:
        rows, cols = w.shape
        rps = -(-rows // (BF16_SUBLANES * (d // tn) * nrow)) * BF16_SUBLANES
        assert rows % rps == 0
        blk = lambda n, i, last=rows // rps - 1: jnp.minimum(n * nrow + i, last)
        in_specs.append(pl.BlockSpec((rps, cols), lambda n, i, blk=blk: (blk(n, i), 0)))
        operands.append(w)
        out_specs.append(pl.BlockSpec((rps, out_cols), lambda n, i, blk=blk: (blk(n, i), 0)))
        out_shape.append(jax.ShapeDtypeStruct((rows, out_cols), BF16))
    down = pl.pallas_call(
        functools.partial(_ffn_down_kernel, mod_base=mod_base, final_norm=final_norm,
                          n_side=len(side_down)),
        grid=(d // tn, nrow),
        in_specs=in_specs,
        out_specs=out_specs,
        out_shape=out_shape,
        compiler_params=pltpu.CompilerParams(
            dimension_semantics=("arbitrary", "arbitrary"), vmem_limit_bytes=VMEM_LIMIT),
        name="ffn_down",
    )(*operands)
    return down[0], tuple(outs[1:]), tuple(down[1:])


def _inproj_kernel(x_ref, mod_ref, gain_ref, cos_ref, sin_ref, win_ref, qg_ref, wuq_ref,
                   kvg_ref, wukv_ref, qk_ref, v_ref, g_ref, qm_ref, km_ref, vm_ref):
    y = _rms(x_ref[...], gain_ref[...])
    h = (y * (1 + mod_ref[4:5, :]) + mod_ref[3:4, :]).astype(BF16)
    cos = cos_ref[...]
    sin = sin_ref[...]

    def proj(c0, c1):
        return jnp.dot(h, win_ref[:, c0:c1], preferred_element_type=F32)

    p_cq = proj(IN_CQ0, IN_CKV0)
    tail = proj(IN_CKV0, IN_COLS_PAD)

    half = IN_QK // 2
    for c0, mult in ((0, 1.0), (half, RET_DK ** -0.5)):
        p = proj(c0, c0 + half)
        for t in range(half // LANES):
            r = _rope(p[:, t * LANES:(t + 1) * LANES], cos, sin)
            qk_ref[:, c0 + t * LANES:c0 + (t + 1) * LANES] = (r * mult).astype(BF16)

    cq = _rms(p_cq, qg_ref[...]).astype(BF16)
    q = jnp.dot(cq, wuq_ref[...], preferred_element_type=F32)
    qscale = (MLA_NOPE + MLA_ROPE) ** -0.5 * LOG2_E
    for hd in range(MLA_HEADS):
        b0 = hd * MLA_QK_PAD
        qm_ref[:, b0:b0 + LANES] = (q[:, b0:b0 + LANES] * qscale).astype(BF16)
        r = _rope(q[:, b0 + LANES:b0 + 2 * LANES], cos, sin)
        qm_ref[:, b0 + LANES:b0 + 2 * LANES] = (r * qscale).astype(BF16)

    ckv = _rms(tail[:, :MLA_KV_RANK], kvg_ref[...]).astype(BF16)
    kpe = _rope(tail[:, MLA_KV_RANK:], cos, sin).astype(BF16)
    kv = jnp.dot(ckv, wukv_ref[...], preferred_element_type=F32)
    for hd in range(MLA_HEADS):
        b0 = hd * (MLA_NOPE + MLA_DV)
        km_ref[:, hd * MLA_QK_PAD:hd * MLA_QK_PAD + LANES] = kv[:, b0:b0 + MLA_NOPE].astype(BF16)
        km_ref[:, hd * MLA_QK_PAD + LANES:(hd + 1) * MLA_QK_PAD] = kpe
        vm_ref[hd * MLA_DV:(hd + 1) * MLA_DV, :] = (
            kv[:, b0 + MLA_NOPE:b0 + MLA_NOPE + MLA_DV].T.astype(BF16))

    v_ref[...] = proj(IN_V0, IN_G0).astype(BF16)
    g_ref[...] = proj(IN_G0, IN_CQ0).astype(BF16)


def _inproj(x, mod, gain, cos, sin, w_in, q_gain, w_uq, kv_gain, w_ukv, *, layer, seq):
    t, d = x.shape
    tm = INPROJ_TM
    const = lambda i: (0, 0)
    row = lambda i: (i, 0)

    def resident(w):
        return pl.BlockSpec((None,) + w.shape[1:], lambda i: (layer, 0, 0),
                            pipeline_mode=pl.Buffered(1))

    widths = [IN_QK, RET_WIDTH, RET_WIDTH, MLA_HEADS * MLA_QK_PAD, MLA_HEADS * MLA_QK_PAD]
    tpb = seq // tm
    sub = ATTN_BLK // tm
    vt_spec = pl.BlockSpec((None, None, MLA_WIDTH, tm),
                           lambda i: (i // tpb, (i % tpb) // sub, 0, i % sub))
    vt_shape = jax.ShapeDtypeStruct((t // seq, seq // ATTN_BLK, MLA_WIDTH, ATTN_BLK), BF16)
    return pl.pallas_call(
        _inproj_kernel,
        grid=(t // tm,),
        in_specs=[
            pl.BlockSpec((tm, d), row),
            pl.BlockSpec((None, None, N_MOD, d), lambda i: (layer, i // (seq // tm), 0, 0)),
            pl.BlockSpec((1, d), const),
            pl.BlockSpec((tm, LANES), row),
            pl.BlockSpec((tm, LANES), row),
            resident(w_in),
            pl.BlockSpec((1, MLA_Q_RANK), const),
            resident(w_uq),
            pl.BlockSpec((1, MLA_KV_RANK), const),
            resident(w_ukv),
        ],
        out_specs=[pl.BlockSpec((tm, w), row) for w in widths] + [vt_spec],
        out_shape=[jax.ShapeDtypeStruct((t, w), BF16) for w in widths] + [vt_shape],
        compiler_params=pltpu.CompilerParams(
            dimension_semantics=("arbitrary",), vmem_limit_bytes=VMEM_LIMIT),
        name="mixer_inproj",
    )(x, mod, gain.reshape(1, d), cos, sin, w_in, q_gain.reshape(1, -1), w_uq,
      kv_gain.reshape(1, -1), w_ukv)


def _retention_kernel(qk_ref, v_ref, g_ref, intra_ref, qdec_ref, kdec_ref, cdec_ref, hmask_ref,
                      gain_ref, o_ref, state_sc, *, chunks):
    @pl.when(pl.program_id(1) == 0)
    def _():
        state_sc[...] = jnp.zeros_like(state_sc)

    kbase = RET_HEADS * RET_DK

    def chunk_body(c, carry):
        rows = pl.ds(pl.multiple_of(c * RET_CHUNK, RET_CHUNK), RET_CHUNK)
        for pair in range(RET_HEADS // 2):
            qf = qk_ref[rows, pair * LANES:(pair + 1) * LANES].astype(F32)
            kp = qk_ref[rows, kbase + pair * LANES:kbase + (pair + 1) * LANES]
            kdt = (kp.astype(F32) * kdec_ref[pair]).T.astype(BF16)
            for hh in range(2):
                hd = 2 * pair + hh
                cols = slice(hd * RET_DV, (hd + 1) * RET_DV)
                vh = v_ref[rows, cols]
                qm = (qf * hmask_ref[hh]).astype(BF16)
                s = lax.dot_general(qm, kp, (((1,), (1,)), ((), ())), preferred_element_type=F32)
                intra = jnp.dot((s * intra_ref[hd]).astype(BF16), vh, preferred_element_type=F32)
                qd = (qf * qdec_ref[hd]).astype(BF16)
                state = state_sc[hd]
                cross = jnp.dot(qd, state.astype(BF16), preferred_element_type=F32)
                state_sc[hd] = state * cdec_ref[hd] + jnp.dot(kdt, vh, preferred_element_type=F32)
                yh = _rms(intra + cross, gain_ref[:, cols])
                o_ref[rows, cols] = (jax.nn.silu(g_ref[rows, cols].astype(F32)) * yh).astype(BF16)
        return carry

    lax.fori_loop(0, chunks, chunk_body, 0)


def _retention_consts():
    c = RET_CHUNK
    log_g = jnp.log1p(-(2.0 ** (-5.0 - jnp.arange(RET_HEADS, dtype=F32))))
    idx = jnp.arange(c, dtype=F32)
    rel = idx[:, None] - idx[None, :]
    intra = jnp.where(rel >= 0, jnp.exp(log_g[:, None, None] * jnp.maximum(rel, 0.0)), 0.0)
    q_decay = jnp.exp(log_g[:, None] * (idx + 1.0))
    k_decay = jnp.exp(log_g[:, None] * (c - 1.0 - idx))
    chunk_decay = jnp.exp(log_g * c)
    own = (np.arange(LANES)[None, :] // RET_DK == np.arange(2)[:, None]).astype(np.float32)
    hmask = jnp.asarray(own).reshape(2, 1, LANES)
    qdec = q_decay[:, :, None] * jnp.asarray(own)[jnp.arange(RET_HEADS) % 2][:, None, :]
    kdec = jnp.repeat(k_decay.reshape(RET_HEADS // 2, 2, c), RET_DK, axis=1).transpose(0, 2, 1)
    cdec = jnp.broadcast_to(chunk_decay[:, None, None], (RET_HEADS, 1, LANES))
    return intra, qdec, kdec, cdec, hmask


def _retention(qk, v, g, gain, *, batch, seq):
    tb = RET_TB
    chunks = tb // RET_CHUNK
    nb = seq // tb
    intra, qdec, kdec, cdec, hmask = _retention_consts()
    row = lambda b, i: (b * nb + i, 0)
    c3 = lambda b, i: (0, 0, 0)
    return pl.pallas_call(
        functools.partial(_retention_kernel, chunks=chunks),
        grid=(batch, nb),
        in_specs=[
            pl.BlockSpec((tb, IN_QK), row),
            pl.BlockSpec((tb, RET_WIDTH), row),
            pl.BlockSpec((tb, RET_WIDTH), row),
            pl.BlockSpec(intra.shape, c3),
            pl.BlockSpec(qdec.shape, c3),
            pl.BlockSpec(kdec.shape, c3),
            pl.BlockSpec(cdec.shape, c3),
            pl.BlockSpec(hmask.shape, c3),
            pl.BlockSpec((1, RET_WIDTH), lambda b, i: (0, 0)),
        ],
        out_specs=pl.BlockSpec((tb, RET_WIDTH), row),
        out_shape=jax.ShapeDtypeStruct((batch * seq, RET_WIDTH), BF16),
        scratch_shapes=[pltpu.VMEM((RET_HEADS, LANES, RET_DV), F32)],
        compiler_params=pltpu.CompilerParams(
            dimension_semantics=("arbitrary", "arbitrary"), vmem_limit_bytes=VMEM_LIMIT),
        name="retention",
    )(qk, v, g, intra, qdec, kdec, cdec, hmask, gain.reshape(1, RET_WIDTH))


def _attn_kernel(q_ref, qnext_ref, k_ref, vt_ref, o_ref, s_sc, acc_sc, m_sc, l_sc):
    qi = pl.program_id(2)
    blk = ATTN_BLK
    acc_sc[...] = jnp.zeros_like(acc_sc)

    def scores(ki, slot, queries=q_ref):
        rows = pl.ds(pl.multiple_of(ki * blk, blk), blk)
        for hd in range(ATTN_HEADS):
            cols = slice(hd * MLA_QK_PAD, (hd + 1) * MLA_QK_PAD)
            s_sc[slot, hd] = lax.dot_general(k_ref[rows, cols], queries[:, cols],
                                             (((1,), (1,)), ((), ())), preferred_element_type=F32)

    def update(ki, slot, stats, masked):
        new_stats = []
        for hd in range(ATTN_HEADS):
            m_prev, l_prev = stats[hd]
            s = s_sc[slot, hd]
            if masked:
                kpos = lax.broadcasted_iota(jnp.int32, s.shape, 0)
                qpos = lax.broadcasted_iota(jnp.int32, s.shape, 1)
                s = jnp.where(kpos <= qpos, s, -jnp.inf)
            m_new = jnp.maximum(m_prev, jnp.max(s, axis=0, keepdims=True))
            p = jnp.exp2(s - m_new)
            alpha = jnp.exp2(m_prev - m_new)
            l_new = alpha * l_prev + jnp.sum(p, axis=0, keepdims=True)
            vt = vt_ref[ki, hd * MLA_DV:(hd + 1) * MLA_DV, :]
            acc_sc[hd] = alpha * acc_sc[hd] + jnp.dot(vt, p.astype(BF16), preferred_element_type=F32)
            new_stats.append((m_new, l_new))
        return tuple(new_stats)

    def load_stats():
        return tuple((m_sc[hd], l_sc[hd]) for hd in range(ATTN_HEADS))

    def store_stats(stats):
        for hd in range(ATTN_HEADS):
            m_sc[hd], l_sc[hd] = stats[hd]

    @pl.when(qi == 0)
    def _():
        scores(0, 0)

    def block_pair(kp, stats):
        scores(2 * kp + 1, 1)
        stats = update(2 * kp, 0, stats, False)
        scores(2 * kp + 2, 0)
        stats = update(2 * kp + 1, 1, stats, False)
        return stats

    init = tuple((jnp.full((1, blk), -jnp.inf, F32), jnp.zeros((1, blk), F32))
                 for _ in range(ATTN_HEADS))
    store_stats(lax.fori_loop(0, qi // 2, block_pair, init))

    @pl.when(qi % 2 == 1)
    def _():
        stats = update(qi - 1, 0, load_stats(), False)
        scores(qi, 1)
        store_stats(stats)

    stats = update(qi, qi % 2, load_stats(), True)
    scores(0, 0, qnext_ref)
    for hd in range(ATTN_HEADS):
        out_t = acc_sc[hd] / stats[hd][1]
        o_ref[:, hd * MLA_DV:(hd + 1) * MLA_DV] = out_t.T.astype(BF16)


def _attention(qm, km, vt, *, batch, seq):
    blk = ATTN_BLK
    nq = seq // blk
    qm = qm.reshape(batch, seq, -1)
    km = km.reshape(batch, seq, -1)
    out = pl.pallas_call(
        _attn_kernel,
        grid=(batch, MLA_HEADS // ATTN_HEADS, nq),
        in_specs=[
            pl.BlockSpec((None, blk, ATTN_HEADS * MLA_QK_PAD), lambda b, h, i: (b, i, h)),
            pl.BlockSpec((None, blk, ATTN_HEADS * MLA_QK_PAD),
                         lambda b, h, i: (b, jnp.minimum(i + 1, nq - 1), h)),
            pl.BlockSpec((None, seq, ATTN_HEADS * MLA_QK_PAD), lambda b, h, i: (b, 0, h)),
            pl.BlockSpec((None, nq, ATTN_HEADS * MLA_DV, blk), lambda b, h, i: (b, 0, h, 0)),
        ],
        out_specs=pl.BlockSpec((None, blk, ATTN_HEADS * MLA_DV), lambda b, h, i: (b, i, h)),
        out_shape=jax.ShapeDtypeStruct((batch, seq, MLA_WIDTH), BF16),
        scratch_shapes=[pltpu.VMEM((2, ATTN_HEADS, blk, blk), F32),
                        pltpu.VMEM((ATTN_HEADS, MLA_DV, blk), F32),
                        pltpu.VMEM((ATTN_HEADS, 1, blk), F32), pltpu.VMEM((ATTN_HEADS, 1, blk), F32)],
        compiler_params=pltpu.CompilerParams(
            dimension_semantics=("arbitrary", "arbitrary", "arbitrary"),
            vmem_limit_bytes=VMEM_LIMIT),
        name="mla_attention",
    )(qm, qm, km, vt)
    return out.reshape(batch * seq, MLA_WIDTH)


def _outproj_kernel(x_ref, mod_ref, yr_ref, ym_ref, w_ref, o_ref):
    y = jnp.dot(yr_ref[...], w_ref[:RET_WIDTH, :], preferred_element_type=F32)
    y += jnp.dot(ym_ref[...], w_ref[RET_WIDTH:, :], preferred_element_type=F32)
    o_ref[...] = x_ref[...] + mod_ref[5:6, :] * y


def _outproj(x, mod, y_r, y_m, w_out, *, layer, seq):
    t, d = x.shape
    tm = OUTPROJ_TM
    row = lambda i: (i, 0)
    return pl.pallas_call(
        _outproj_kernel,
        grid=(t // tm,),
        in_specs=[
            pl.BlockSpec((tm, d), row),
            pl.BlockSpec((None, None, N_MOD, d), lambda i: (layer, i // (seq // tm), 0, 0)),
            pl.BlockSpec((tm, RET_WIDTH), row),
            pl.BlockSpec((tm, MLA_WIDTH), row),
            pl.BlockSpec((None,) + w_out.shape[1:], lambda i: (layer, 0, 0),
                         pipeline_mode=pl.Buffered(1)),
        ],
        out_specs=pl.BlockSpec((tm, d), row),
        out_shape=jax.ShapeDtypeStruct((t, d), F32),
        compiler_params=pltpu.CompilerParams(
            dimension_semantics=("arbitrary",), vmem_limit_bytes=VMEM_LIMIT),
        name="mixer_outproj",
    )(x, mod, y_r, y_m, w_out)


def kernel(x, c, positions, w_ada, b_ada, norm_ffn1, ffn1_w_gu, ffn1_w_down, norm_mix, w_in,
           ret_norm_g, q_norm_g, w_uq, kv_norm_g, w_ukv, w_out, norm_ffn2, ffn2_w_gu,
           ffn2_w_down, final_norm):
    batch, seq, d = x.shape
    depth = w_ada.shape[0]
    xt = x.reshape(batch * seq, d)

    c_pad = jnp.pad(c, ((0, 8 - batch), (0, 0)))
    mod = _adaln_mod(c_pad, w_ada, b_ada)[:, :batch].reshape(depth, batch, N_MOD, d)
    cos, sin = _rope_tables(positions)

    w_uq_b = jnp.pad(w_uq.reshape(depth, MLA_Q_RANK, MLA_HEADS, MLA_NOPE + MLA_ROPE),
                     ((0, 0), (0, 0), (0, 0), (0, MLA_QK_PAD - MLA_NOPE - MLA_ROPE))
                     ).reshape(depth, MLA_Q_RANK, MLA_HEADS * MLA_QK_PAD).astype(BF16)
    w_in_b = jnp.pad(w_in, ((0, 0), (0, 0), (0, IN_COLS_PAD - w_in.shape[-1]))).astype(BF16)
    ffn_f32 = [(ffn1_w_gu, ffn1_w_down), (ffn2_w_gu, ffn2_w_down)]
    ffn_gain = [norm_ffn1, norm_ffn2]
    ffn_w = (_to_bf16(ffn1_w_gu, layer=0), _to_bf16(ffn1_w_down, layer=0))
    mixer_f32 = [(w.reshape(-1, w.shape[-1]), w.shape[-1]) for w in (w_out, w_ukv)]

    def ffn(xt, weights, l, which):
        first = l == 0 and which == 0
        last = l == depth - 1 and which == 1
        nxt_l, nxt_which = (l, 1) if which == 0 else (l + 1, 0)
        side = None if last else ffn_f32[nxt_which] + (nxt_l,)
        return _ffn(xt, mod, ffn_gain[which][l], weights[0], weights[1], final_norm, side,
                    mixer_f32 if first else (), layer=l, mod_base=6 * which, final_norm=last,
                    seq=seq)

    xt, ffn_w, mixer_b = ffn(xt, ffn_w, 0, 0)
    w_out_b, w_ukv_b = (w.reshape(depth, -1, w.shape[-1]) for w in mixer_b)
    for l in range(depth):
        if l > 0:
            xt, ffn_w, _ = ffn(xt, ffn_w, l, 0)
        qk, v_r, g_r, qm, km, vt = _inproj(xt, mod, norm_mix[l], cos, sin, w_in_b, q_norm_g[l],
                                           w_uq_b, kv_norm_g[l], w_ukv_b, layer=l, seq=seq)
        y_r = _retention(qk, v_r, g_r, ret_norm_g[l], batch=batch, seq=seq)
        y_m = _attention(qm, km, vt, batch=batch, seq=seq)
        xt = _outproj(xt, mod, y_r, y_m, w_out_b, layer=l, seq=seq)
        xt, ffn_w, _ = ffn(xt, ffn_w, l, 1)
    return xt.reshape(batch, seq, d)
```
